```python
import jax, jax.numpy as jnp
from jax import lax
import numpy as np

D_MODEL = 1024
BATCH = 16
SEQ = 2048
DEPTH = 2

CTX_LEN = 256
GRID_W = 64

MIX_WIDTH = D_MODEL
CONV_CH = MIX_WIDTH // 4
CONV_K = 31
RET_HEADS = 4
RET_QK_DIM = 32
RET_V_DIM = 64
RET_CHUNK = 128
MLA_HEADS = 8
MLA_NOPE_DIM = 64
MLA_ROPE_DIM = 32
MLA_V_DIM = 64
MLA_Q_RANK = 256
MLA_KV_RANK = 128
Q_BLOCK = 128
ROPE_BASE = 10000.0
N_EXPERTS = 64
TOP_K = 6
EXPERT_DIM = 256
ROUTED_SCALE = 2.5
MOE_BLOCK = 128
NORM_EPS = 1e-6

IN_WIDTHS = (2 * CONV_CH,
             RET_HEADS * RET_QK_DIM, RET_HEADS * RET_QK_DIM, RET_HEADS * RET_V_DIM, RET_HEADS * RET_V_DIM,
             MLA_Q_RANK, MLA_KV_RANK, MLA_ROPE_DIM)
IN_SPLITS = tuple(sum(IN_WIDTHS[:i + 1]) for i in range(len(IN_WIDTHS) - 1))
IN_COLS = sum(IN_WIDTHS)

kernel_name = 'hybrid_conv_retention_mla_moe_dit'


def rms_norm(x, w):
    xf = x.astype(jnp.float32)
    y = xf * lax.rsqrt(jnp.mean(xf * xf, axis=-1, keepdims=True) + NORM_EPS)
    return y.astype(x.dtype) * w


def layer_norm(x, w, b):
    xf = x.astype(jnp.float32)
    mu = jnp.mean(xf, axis=-1, keepdims=True)
    var = jnp.mean(jnp.square(xf - mu), axis=-1, keepdims=True)
    return ((xf - mu) * lax.rsqrt(var + NORM_EPS)).astype(x.dtype) * w + b


def modulate(h, shift, scale):
    return h * (1 + scale) + shift


def axial_rope_tables(rows, dim):
    pos_r = jnp.repeat(jnp.arange(rows, dtype=jnp.float32), GRID_W)
    pos_c = jnp.tile(jnp.arange(GRID_W, dtype=jnp.float32), rows)
    n_freq = dim // 4
    inv = ROPE_BASE ** (-jnp.arange(n_freq, dtype=jnp.float32) / n_freq)
    ang = jnp.concatenate([pos_r[:, None] * inv, pos_c[:, None] * inv], axis=-1)
    return jnp.cos(ang), jnp.sin(ang)


def apply_rope(x, cos, sin):
    x1, x2 = jnp.split(x, 2, axis=-1)
    cos = cos.astype(x.dtype)
    sin = sin.astype(x.dtype)
    return jnp.concatenate([x1 * cos - x2 * sin, x1 * sin + x2 * cos], axis=-1)


def conformer_conv(u, conv_w, conv_b, ln_w, ln_b):
    a, g = jnp.split(u, 2, axis=-1)
    h = a * jax.nn.sigmoid(g)
    h = lax.conv_general_dilated(h, conv_w[:, None, :], window_strides=(1,),
                                 padding=[(CONV_K // 2, CONV_K // 2)],
                                 dimension_numbers=('NWC', 'WIO', 'NWC'),
                                 feature_group_count=CONV_CH) + conv_b
    return jax.nn.silu(layer_norm(h, ln_w, ln_b))


def retention_scan(q, k, v, log_gamma, s0):
    b, n_tok, h, dk = q.shape
    dv = v.shape[-1]
    n_chunks = n_tok // RET_CHUNK

    def to_chunks(t):
        return t.reshape(b, n_chunks, RET_CHUNK, h, t.shape[-1]).transpose(1, 0, 3, 2, 4)

    idx = jnp.arange(RET_CHUNK, dtype=jnp.float32)
    rel = idx[:, None] - idx[None, :]
    decay_in = jnp.where(rel >= 0, jnp.exp(log_gamma[:, None, None] * jnp.maximum(rel, 0.0)), 0.0)
    decay_q = jnp.exp(log_gamma[:, None] * (idx + 1.0))[:, :, None]
    decay_k = jnp.exp(log_gamma[:, None] * (RET_CHUNK - 1.0 - idx))[:, :, None]
    decay_chunk = jnp.exp(log_gamma * RET_CHUNK)[:, None, None]

    def step(s, qkv):
        qc, kc, vc = qkv
        inner = jnp.einsum('bhqd,bhkd->bhqk', qc, kc) * decay_in
        y = (jnp.einsum('bhqk,bhkv->bhqv', inner, vc)
             + jnp.einsum('bhqd,bhdv->bhqv', qc * decay_q, s))
        s = s * decay_chunk + jnp.einsum('bhkd,bhkv->bhdv', kc * decay_k, vc)
        return s, y

    s_final, ys = lax.scan(step, s0, (to_chunks(q), to_chunks(k), to_chunks(v)))
    return ys.transpose(1, 0, 3, 2, 4).reshape(b, n_tok, h, dv), s_final


def retention_qkv(zq, zk, zv, rope):
    b, n_tok, _ = zq.shape
    q = zq.reshape(b, n_tok, RET_HEADS, RET_QK_DIM).astype(jnp.float32)
    k = zk.reshape(b, n_tok, RET_HEADS, RET_QK_DIM).astype(jnp.float32) * (RET_QK_DIM ** -0.5)
    v = zv.reshape(b, n_tok, RET_HEADS, RET_V_DIM).astype(jnp.float32)
    if rope is not None:
        cos, sin = rope
        q = apply_rope(q, cos[None, :, None, :], sin[None, :, None, :])
        k = apply_rope(k, cos[None, :, None, :], sin[None, :, None, :])
    return q, k, v


def retention_out(y, g, gn_w):
    b, n_tok, h, dv = y.shape
    mu = jnp.mean(y, axis=-1, keepdims=True)
    var = jnp.mean(jnp.square(y - mu), axis=-1, keepdims=True)
    yn = ((y - mu) * lax.rsqrt(var + NORM_EPS)).reshape(b, n_tok, h * dv).astype(g.dtype)
    return jax.nn.silu(g) * (yn * gn_w)


def mla_q(cq, q_norm_w, w_uq, rope):
    b, n_tok, _ = cq.shape
    q = (rms_norm(cq, q_norm_w) @ w_uq).reshape(b, n_tok, MLA_HEADS, MLA_NOPE_DIM + MLA_ROPE_DIM)
    if rope is not None:
        cos, sin = rope
        q = jnp.concatenate([q[..., :MLA_NOPE_DIM],
                             apply_rope(q[..., MLA_NOPE_DIM:], cos[None, :, None, :], sin[None, :, None, :])], axis=-1)
    return q


def mla_kv(ckv, k_rope, kv_norm_w, w_ukv, rope):
    b, n_tok, _ = ckv.shape
    kv = (rms_norm(ckv, kv_norm_w) @ w_ukv).reshape(b, n_tok, MLA_HEADS, MLA_NOPE_DIM + MLA_V_DIM)
    k_nope, v = kv[..., :MLA_NOPE_DIM], kv[..., MLA_NOPE_DIM:]
    if rope is not None:
        cos, sin = rope
        k_rope = apply_rope(k_rope, cos[None], sin[None])
    k = jnp.concatenate([k_nope, jnp.broadcast_to(k_rope[:, :, None, :], (b, n_tok, MLA_HEADS, MLA_ROPE_DIM))], axis=-1)
    return k, v


def block_attention(q, k, v):
    b, n_q, h, dq = q.shape
    n_blocks = n_q // Q_BLOCK
    scale = dq ** -0.5
    qb = q.reshape(b, n_blocks, Q_BLOCK, h, dq).swapaxes(0, 1)

    def one_block(qi):
        s = jnp.einsum('bqhd,bkhd->bhqk', qi, k).astype(jnp.float32) * scale
        p = jax.nn.softmax(s, axis=-1).astype(v.dtype)
        return jnp.einsum('bhqk,bkhd->bqhd', p, v)

    o = lax.map(one_block, qb)
    return o.swapaxes(0, 1).reshape(b, n_q, h * v.shape[-1])


def token_mixer(hl, hc, w_in, conv_w, conv_b, conv_ln_w, conv_ln_b, ret_decay_logit, ret_gn_w,
                q_norm_w, w_uq, kv_norm_w, w_ukv, w_out, rope_ret, rope_mla, with_ctx_out):
    zl = jnp.split(hl @ w_in, IN_SPLITS, axis=-1)
    zc = jnp.split(hc @ w_in, IN_SPLITS, axis=-1)

    conv_l = conformer_conv(zl[0], conv_w, conv_b, conv_ln_w, conv_ln_b)

    ql, kl, vl = retention_qkv(zl[1], zl[2], zl[3], rope_ret)
    qc, kc, vc = retention_qkv(zc[1], zc[2], zc[3], None)
    log_gamma = jax.nn.log_sigmoid(ret_decay_logit.astype(jnp.float32))
    s0 = jnp.zeros((hl.shape[0], RET_HEADS, RET_QK_DIM, RET_V_DIM), jnp.float32)
    flip = lambda t: t[:, ::-1]
    yc_f, sc_f = retention_scan(qc, kc, vc, log_gamma[0], s0)
    yc_b, sc_b = retention_scan(flip(qc), flip(kc), flip(vc), log_gamma[1], s0)
    yl_f, _ = retention_scan(ql, kl, vl, log_gamma[0], sc_f)
    yl_b, _ = retention_scan(flip(ql), flip(kl), flip(vl), log_gamma[1], sc_b)
    ret_l = retention_out(yl_f + flip(yl_b), zl[4], ret_gn_w)

    k_ctx, v_ctx = mla_kv(zc[6], zc[7], kv_norm_w, w_ukv, None)
    k_lat, v_lat = mla_kv(zl[6], zl[7], kv_norm_w, w_ukv, rope_mla)
    q_lat = mla_q(zl[5], q_norm_w, w_uq, rope_mla)
    mla_l = block_attention(q_lat, jnp.concatenate([k_ctx, k_lat], axis=1), jnp.concatenate([v_ctx, v_lat], axis=1))

    yl = jnp.concatenate([conv_l, ret_l, mla_l], axis=-1) @ w_out
    if not with_ctx_out:
        return yl, None
    conv_c = conformer_conv(zc[0], conv_w, conv_b, conv_ln_w, conv_ln_b)
    ret_c = retention_out(yc_f + flip(yc_b), zc[4], ret_gn_w)
    mla_c = block_attention(mla_q(zc[5], q_norm_w, w_uq, None), k_ctx, v_ctx)
    yc = jnp.concatenate([conv_c, ret_c, mla_c], axis=-1) @ w_out
    return yl, yc


def swiglu(h, w_gate, w_up, w_down):
    return (jax.nn.silu(h @ w_gate) * (h @ w_up)) @ w_down


def routed_experts(h, expert_idx, expert_w, e_gate, e_up, e_down):
    n_tok, d = h.shape
    n_assign = n_tok * TOP_K
    n_blocks = -(-n_assign // MOE_BLOCK) + N_EXPERTS
    flat_e = expert_idx.reshape(-1)
    flat_tok = jnp.repeat(jnp.arange(n_tok, dtype=jnp.int32), TOP_K)
    flat_w = expert_w.reshape(-1)
    order = jnp.argsort(flat_e)
    sorted_e = flat_e[order]
    counts = jnp.bincount(flat_e, length=N_EXPERTS)
    padded = (counts + MOE_BLOCK - 1) // MOE_BLOCK * MOE_BLOCK
    pad_end = jnp.cumsum(padded)
    pad_start = pad_end - padded
    start = jnp.cumsum(counts) - counts
    dest = pad_start[sorted_e] + jnp.arange(n_assign, dtype=jnp.int32) - start[sorted_e]
    slot_tok = jnp.full((n_blocks * MOE_BLOCK,), n_tok, jnp.int32).at[dest].set(flat_tok[order])
    slot_w = jnp.zeros((n_blocks * MOE_BLOCK,), h.dtype).at[dest].set(flat_w[order])
    block_e = jnp.minimum(jnp.searchsorted(pad_end, jnp.arange(n_blocks, dtype=jnp.int32) * MOE_BLOCK, side='right'),
                          N_EXPERTS - 1)
    h_pad = jnp.concatenate([h, jnp.zeros((1, d), h.dtype)], axis=0)

    def step(y, blk):
        tok, wt, e = blk
        out = swiglu(h_pad[tok], e_gate[e], e_up[e], e_down[e])
        return y.at[tok].add(out * wt[:, None]), None

    y, _ = lax.scan(step, jnp.zeros((n_tok + 1, d), h.dtype),
                    (slot_tok.reshape(n_blocks, MOE_BLOCK), slot_w.reshape(n_blocks, MOE_BLOCK), block_e))
    return y[:n_tok]


def moe_ffn(h, router_w, router_b, e_gate, e_up, e_down, s_gate, s_up, s_down):
    scores = jax.nn.sigmoid((h @ router_w).astype(jnp.float32))
    _, idx = lax.top_k(scores + router_b.astype(jnp.float32), TOP_K)
    w = jnp.take_along_axis(scores, idx, axis=-1)
    w = (w / jnp.sum(w, axis=-1, keepdims=True) * ROUTED_SCALE).astype(h.dtype)
    return routed_experts(h, idx, w, e_gate, e_up, e_down) + swiglu(h, s_gate, s_up, s_down)


def setup_inputs(seed: int = 0) -> dict:
    key = jax.random.key(seed)
    ks = jax.random.split(key, 32)
    f32 = jnp.float32
    nrm = lambda k, shape, scale: jax.random.normal(k, shape, f32) * scale
    gain = lambda k, shape: 1.0 + 0.02 * jax.random.normal(k, shape, f32)
    ret_logit0 = jnp.log(2.0 ** (5.0 + jnp.arange(RET_HEADS, dtype=f32)) - 1.0)
    return {
        'x': nrm(ks[0], (BATCH, SEQ, D_MODEL), 1.0),
        'c': nrm(ks[1], (BATCH, D_MODEL), 1.0),
        'ctx': nrm(ks[2], (BATCH, CTX_LEN, D_MODEL), 1.0),
        'c_ctx': nrm(ks[3], (D_MODEL,), 1.0),
        'mod_w': nrm(ks[4], (DEPTH, D_MODEL, 6 * D_MODEL), 0.5 * D_MODEL ** -0.5),
        'mod_b': nrm(ks[5], (DEPTH, 6 * D_MODEL), 0.02),
        'norm1_w': gain(ks[6], (DEPTH, D_MODEL)),
        'w_in': nrm(ks[7], (DEPTH, D_MODEL, IN_COLS), D_MODEL ** -0.5),
        'conv_w': nrm(ks[8], (DEPTH, CONV_K, CONV_CH), CONV_K ** -0.5),
        'conv_b': nrm(ks[9], (DEPTH, CONV_CH), 0.02),
        'conv_ln_w': gain(ks[10], (DEPTH, CONV_CH)),
        'conv_ln_b': nrm(ks[11], (DEPTH, CONV_CH), 0.02),
        'ret_decay_logit': ret_logit0 + 0.1 * jax.random.normal(ks[12], (DEPTH, 2, RET_HEADS), f32),
        'ret_gn_w': gain(ks[13], (DEPTH, RET_HEADS * RET_V_DIM)),
        'q_norm_w': gain(ks[14], (DEPTH, MLA_Q_RANK)),
        'w_uq': nrm(ks[15], (DEPTH, MLA_Q_RANK, MLA_HEADS * (MLA_NOPE_DIM + MLA_ROPE_DIM)), MLA_Q_RANK ** -0.5),
        'kv_norm_w': gain(ks[16], (DEPTH, MLA_KV_RANK)),
        'w_ukv': nrm(ks[17], (DEPTH, MLA_KV_RANK, MLA_HEADS * (MLA_NOPE_DIM + MLA_V_DIM)), MLA_KV_RANK ** -0.5),
        'w_out': nrm(ks[18], (DEPTH, MIX_WIDTH, D_MODEL), MIX_WIDTH ** -0.5),
        'norm2_w': gain(ks[19], (DEPTH, D_MODEL)),
        'router_w': nrm(ks[20], (DEPTH, D_MODEL, N_EXPERTS), D_MODEL ** -0.5),
        'router_b': nrm(ks[21], (DEPTH, N_EXPERTS), 0.01),
        'exp_w_gate': nrm(ks[22], (DEPTH, N_EXPERTS, D_MODEL, EXPERT_DIM), D_MODEL ** -0.5),
        'exp_w_up': nrm(ks[23], (DEPTH, N_EXPERTS, D_MODEL, EXPERT_DIM), D_MODEL ** -0.5),
        'exp_w_down': nrm(ks[24], (DEPTH, N_EXPERTS, EXPERT_DIM, D_MODEL), EXPERT_DIM ** -0.5),
        'sh_w_gate': nrm(ks[25], (DEPTH, D_MODEL, EXPERT_DIM), D_MODEL ** -0.5),
        'sh_w_up': nrm(ks[26], (DEPTH, D_MODEL, EXPERT_DIM), D_MODEL ** -0.5),
        'sh_w_down': nrm(ks[27], (DEPTH, EXPERT_DIM, D_MODEL), EXPERT_DIM ** -0.5),
        'final_norm_w': gain(ks[28], (D_MODEL,)),
    }


def reference(x, c, ctx, c_ctx, mod_w, mod_b, norm1_w, w_in, conv_w, conv_b, conv_ln_w, conv_ln_b,
              ret_decay_logit, ret_gn_w, q_norm_w, w_uq, kv_norm_w, w_ukv, w_out, norm2_w,
              router_w, router_b, exp_w_gate, exp_w_up, exp_w_down, sh_w_gate, sh_w_up, sh_w_down,
              final_norm_w):
    b, n_lat, d = x.shape
    n_ctx = ctx.shape[1]
    rows = n_lat // GRID_W
    rope_ret = axial_rope_tables(rows, RET_QK_DIM)
    rope_mla = axial_rope_tables(rows, MLA_ROPE_DIM)
    xl, xc = x, ctx
    for i in range(DEPTH):
        last = i == DEPTH - 1
        mod_l = jax.nn.silu(c) @ mod_w[i] + mod_b[i]
        mod_c = jax.nn.silu(c_ctx) @ mod_w[i] + mod_b[i]
        sh1, sc1, g1, sh2, sc2, g2 = jnp.split(mod_l[:, None, :], 6, axis=-1)
        csh1, csc1, cg1, csh2, csc2, cg2 = jnp.split(mod_c, 6)

        hl = modulate(rms_norm(xl, norm1_w[i]), sh1, sc1)
        hc = modulate(rms_norm(xc, norm1_w[i]), csh1, csc1)
        yl, yc = token_mixer(hl, hc, w_in[i], conv_w[i], conv_b[i], conv_ln_w[i], conv_ln_b[i],
                             ret_decay_logit[i], ret_gn_w[i], q_norm_w[i], w_uq[i], kv_norm_w[i], w_ukv[i],
                             w_out[i], rope_ret, rope_mla, not last)
        xl = xl + g1 * yl

        hl2 = modulate(rms_norm(xl, norm2_w[i]), sh2, sc2).reshape(b * n_lat, d)
        if last:
            out = moe_ffn(hl2, router_w[i], router_b[i], exp_w_gate[i], exp_w_up[i], exp_w_down[i],
                          sh_w_gate[i], sh_w_up[i], sh_w_down[i])
            xl = xl + g2 * out.reshape(b, n_lat, d)
        else:
            xc = xc + cg1 * yc
            hc2 = modulate(rms_norm(xc, norm2_w[i]), csh2, csc2).reshape(b * n_ctx, d)
            out = moe_ffn(jnp.concatenate([hl2, hc2], axis=0), router_w[i], router_b[i], exp_w_gate[i],
                          exp_w_up[i], exp_w_down[i], sh_w_gate[i], sh_w_up[i], sh_w_down[i])
            xl = xl + g2 * out[:b * n_lat].reshape(b, n_lat, d)
            xc = xc + cg2 * out[b * n_lat:].reshape(b, n_ctx, d)
    return rms_norm(xl, final_norm_w)
```

```python
import functools

import jax
import jax.numpy as jnp
from jax import lax
from jax.experimental import pallas as pl
from jax.experimental.pallas import tpu as pltpu

F32 = jnp.float32
BF16 = jnp.bfloat16

D_MODEL = 1024
GRID_W = 64
CONV_CH = 256
CONV_K = 31
RET_HEADS = 4
RET_QK_DIM = 32
RET_V_DIM = 64
RET_CHUNK = 128
MLA_HEADS = 8
MLA_NOPE_DIM = 64
MLA_ROPE_DIM = 32
MLA_V_DIM = 64
MLA_Q_RANK = 256
MLA_KV_RANK = 128
ROPE_BASE = 10000.0
N_EXPERTS = 64
TOP_K = 6
EXPERT_DIM = 256
ROUTED_SCALE = 2.5
NORM_EPS = 1e-6

LANES = 128
SUBLANES = 8
HEAD_PAD = 128
RET_W = RET_HEADS * RET_QK_DIM
RET_VW = RET_HEADS * RET_V_DIM
IN_COLS_PAD = 2 * CONV_CH + 2 * RET_W + 2 * RET_VW + MLA_Q_RANK + MLA_KV_RANK + HEAD_PAD
MOE_BLOCK = 128
MOE_STRIDE = MOE_BLOCK + SUBLANES
VMEM_LIMIT = 56 * 1024 * 1024


def _cp(sem, vmem=None):
    return pltpu.CompilerParams(dimension_semantics=sem, vmem_limit_bytes=vmem)


def _dot(a, b):
    return jnp.dot(a, b, preferred_element_type=F32)


def _split_bf16(a):
    hi = a.astype(BF16)
    lo = (a - hi.astype(F32)).astype(BF16)
    return hi, lo


def _sigmoid(x):
    return 1.0 / (1.0 + jnp.exp(-x))


def _silu(x):
    return x * _sigmoid(x)


def _mod_kernel(c_ref, w_ref, b_ref, o_ref):
    a_hi, a_lo = _split_bf16(_silu(c_ref[...]))
    w_hi, w_lo = _split_bf16(w_ref[...])
    o_ref[...] = _dot(a_hi, w_hi) + _dot(a_lo, w_hi) + _dot(a_hi, w_lo) + b_ref[...]


def _modulation(cc, w, b):
    rows, d = cc.shape
    n = w.shape[1]
    bn = 1536
    return pl.pallas_call(
        _mod_kernel,
        grid=(n // bn,),
        in_specs=[pl.BlockSpec((rows, d), lambda j: (0, 0)),
                  pl.BlockSpec((d, bn), lambda j: (0, j)),
                  pl.BlockSpec((1, bn), lambda j: (0, j))],
        out_specs=pl.BlockSpec((rows, bn), lambda j: (0, j)),
        out_shape=jax.ShapeDtypeStruct((rows, n), F32),
        compiler_params=_cp(("arbitrary",), VMEM_LIMIT),
        name="modulation",
    )(cc, w, b.reshape(1, n))


def _rms_mod(x, nw, sh, sc):
    var = jnp.mean(x * x, axis=-1, keepdims=True)
    h = (x * lax.rsqrt(var + NORM_EPS)) * nw
    return h * (1.0 + sc) + sh


def _inproj_kernel(x_ref, nw_ref, sh_ref, sc_ref, w_ref, u_ref, r_ref, m_ref):
    h = _rms_mod(x_ref[0], nw_ref[...], sh_ref[0], sc_ref[0])
    z = _dot(h.astype(BF16), w_ref[...])
    c0 = 2 * CONV_CH
    c1 = c0 + 2 * RET_W + 2 * RET_VW
    u_ref[0] = z[:, :c0]
    r_ref[0] = z[:, c0:c1]
    m_ref[0] = z[:, c1:]


def _inproj(x, nw, sh, sc, w_pad):
    b, n, d = x.shape
    t = min(512, n)
    wu, wr, wm = 2 * CONV_CH, 2 * RET_W + 2 * RET_VW, MLA_Q_RANK + MLA_KV_RANK + HEAD_PAD
    tok = lambda w: pl.BlockSpec((1, t, w), lambda i, j: (i, j, 0))
    per_b = pl.BlockSpec((1, 1, d), lambda i, j: (i, 0, 0))
    return pl.pallas_call(
        _inproj_kernel,
        grid=(b, n // t),
        in_specs=[tok(d), pl.BlockSpec((1, d), lambda i, j: (0, 0)), per_b, per_b,
                  pl.BlockSpec((d, IN_COLS_PAD), lambda i, j: (0, 0))],
        out_specs=[tok(wu), tok(wr), tok(wm)],
        out_shape=[jax.ShapeDtypeStruct((b, n, wu), F32),
                   jax.ShapeDtypeStruct((b, n, wr), F32),
                   jax.ShapeDtypeStruct((b, n, wm), F32)],
        compiler_params=_cp(("parallel", "parallel"), VMEM_LIMIT),
        name="norm1_inproj",
    )(x, nw.reshape(1, d), sh, sc, w_pad)


_CONV_PAD = 16
_CONV_ROWS = 128


def _conv_kernel(u_ref, cw_ref, cb_ref, lw_ref, lb_ref, o_ref, hp_ref, *, n):
    c = CONV_CH
    hp_ref[0:_CONV_PAD, :] = jnp.zeros((_CONV_PAD, c), F32)
    hp_ref[n + _CONV_PAD:n + 2 * _CONV_PAD, :] = jnp.zeros((_CONV_PAD, c), F32)

    def glu(i, carry):
        r = pl.multiple_of(i * _CONV_ROWS, _CONV_ROWS)
        u = u_ref[0, pl.ds(r, _CONV_ROWS), :]
        hp_ref[pl.ds(r + _CONV_PAD, _CONV_ROWS), :] = u[:, :c] * _sigmoid(u[:, c:])
        return carry

    lax.fori_loop(0, n // _CONV_ROWS, glu, 0)

    def conv(i, carry):
        r = pl.multiple_of(i * _CONV_ROWS, _CONV_ROWS)
        acc = jnp.zeros((_CONV_ROWS, c), F32)
        base = _CONV_PAD - CONV_K // 2
        for q in range((base + CONV_K - 1) // SUBLANES + 1):
            win = hp_ref[pl.ds(r + q * SUBLANES, _CONV_ROWS + SUBLANES), :]
            for s in range(SUBLANES):
                k = q * SUBLANES + s - base
                if 0 <= k < CONV_K:
                    acc = acc + cw_ref[k:k + 1, :] * win[s:s + _CONV_ROWS, :]
        hcv = acc + cb_ref[...]
        mu = jnp.mean(hcv, axis=-1, keepdims=True)
        dlt = hcv - mu
        var = jnp.mean(dlt * dlt, axis=-1, keepdims=True)
        y = (dlt * lax.rsqrt(var + NORM_EPS)) * lw_ref[...] + lb_ref[...]
        o_ref[0, pl.ds(r, _CONV_ROWS), :] = _silu(y)
        return carry

    lax.fori_loop(0, n // _CONV_ROWS, conv, 0)


def _conv(u, cw, cb, lw, lb):
    b, n, _ = u.shape
    c = CONV_CH
    vec = pl.BlockSpec((1, c), lambda i: (0, 0))
    return pl.pallas_call(
        functools.partial(_conv_kernel, n=n),
        grid=(b,),
        in_specs=[pl.BlockSpec((1, n, 2 * c), lambda i: (i, 0, 0)),
                  pl.BlockSpec((CONV_K, c), lambda i: (0, 0)), vec, vec, vec],
        out_specs=pl.BlockSpec((1, n, c), lambda i: (i, 0, 0)),
        out_shape=jax.ShapeDtypeStruct((b, n, c), F32),
        scratch_shapes=[pltpu.VMEM((n + 2 * _CONV_PAD, c), F32)],
        compiler_params=_cp(("parallel",), VMEM_LIMIT),
        name="conformer_conv",
    )(u, cw, cb.reshape(1, c), lw.reshape(1, c), lb.reshape(1, c))


def _rope_partner(x, group, lo):
    half = 16
    lane = lax.broadcasted_iota(jnp.int32, x.shape, 1) % group
    first = (lane >= lo) & (lane < lo + half)
    return jnp.where(first, pltpu.roll(x, LANES - half, 1), pltpu.roll(x, half, 1))


def _ret_kernel(lg_ref, rl_ref, rc_ref, cos_ref, sin_ref, gn_ref, avg_ref, ol_ref, oc_ref,
                q_s, k_s, yl_s, yc_s, dst_s, dq_s, dk_s, dch_s, *, n_lat, n_ctx, ctx_out):
    ch = RET_CHUNK
    lane_q = lax.broadcasted_iota(jnp.int32, (1, RET_W), 1) // RET_QK_DIM
    lane_v = lax.broadcasted_iota(jnp.int32, (1, RET_VW), 1) // RET_V_DIM
    row_h = lax.broadcasted_iota(jnp.int32, (RET_W, 1), 0) // RET_QK_DIM
    bd = (row_h == lane_v).astype(F32)
    qmask = [(lane_q == h).astype(F32) for h in range(RET_HEADS)]
    vmask = [(lane_v == h).astype(F32) for h in range(RET_HEADS)]

    ri = lax.broadcasted_iota(jnp.int32, (ch, ch), 0).astype(F32)
    ci = lax.broadcasted_iota(jnp.int32, (ch, ch), 1).astype(F32)
    rowi = lax.broadcasted_iota(jnp.int32, (ch, 1), 0).astype(F32)
    for d in range(2):
        lg_lane = jnp.zeros((1, RET_W), F32)
        lg_row = jnp.zeros((RET_W, 1), F32)
        for h in range(RET_HEADS):
            lg = lg_ref[d * RET_HEADS + h]
            lg_lane = jnp.where(lane_q == h, lg, lg_lane)
            lg_row = jnp.where(row_h == h, lg, lg_row)
            rel = (ri - ci) if d == 0 else (ci - ri)
            dst_s[d, h * ch:(h + 1) * ch, :] = jnp.where(
                rel >= 0, jnp.exp(lg * jnp.maximum(rel, 0.0)), 0.0)
        if d == 0:
            dq_s[d] = jnp.exp(lg_lane * (rowi + 1.0))
            dk_s[d] = jnp.exp(lg_lane * (ch - 1.0 - rowi))
        else:
            dq_s[d] = jnp.exp(lg_lane * (ch - rowi))
            dk_s[d] = jnp.exp(lg_lane * rowi)
        dch_s[d] = jnp.exp(lg_row * float(ch)) * jnp.ones((1, RET_VW), F32)

    kscale = RET_QK_DIM ** -0.5

    def stage(src_ref, n, rope):
        def body(i, carry):
            r = pl.multiple_of(i * ch, ch)
            q = src_ref[0, pl.ds(r, ch), 0:RET_W]
            k = src_ref[0, pl.ds(r, ch), RET_W:2 * RET_W] * kscale
            if rope:
                cs = cos_ref[pl.ds(r, ch), :]
                sn = sin_ref[pl.ds(r, ch), :]
                q = q * cs + _rope_partner(q, RET_QK_DIM, 0) * sn
                k = k * cs + _rope_partner(k, RET_QK_DIM, 0) * sn
            q_s[pl.ds(r, ch), :] = q
            k_s[pl.ds(r, ch), :] = k
            return carry
        lax.fori_loop(0, n // ch, body, 0)

    def scan(src_ref, y_ref, n, d, s0, accumulate):
        nchunks = n // ch

        def body(i, s):
            c = i if d == 0 else nchunks - 1 - i
            r = pl.multiple_of(c * ch, ch)
            qc = q_s[pl.ds(r, ch), :]
            kc = k_s[pl.ds(r, ch), :]
            vc = src_ref[0, pl.ds(r, ch), 2 * RET_W:2 * RET_W + RET_VW].astype(BF16)
            qst = jnp.concatenate([qc * qmask[h] for h in range(RET_HEADS)], axis=0).astype(BF16)
            inner = lax.dot_general(qst, kc.astype(BF16), (((1,), (1,)), ((), ())),
                                    preferred_element_type=F32) * dst_s[d]
            o = _dot(inner.astype(BF16), vc)
            y = _dot((qc * dq_s[d]).astype(BF16), s.astype(BF16))
            for h in range(RET_HEADS):
                y = y + o[h * ch:(h + 1) * ch, :] * vmask[h]
            kd_t = (kc * dk_s[d]).T.astype(BF16)
            s_new = s * dch_s[d] + _dot(kd_t, vc) * bd
            if accumulate:
                y_ref[pl.ds(r, ch), :] = y_ref[pl.ds(r, ch), :] + y
            else:
                y_ref[pl.ds(r, ch), :] = y
            return s_new

        return lax.fori_loop(0, nchunks, body, s0)

    def finish(src_ref, y_ref, out_ref, n):
        def body(i, carry):
            r = pl.multiple_of(i * ch, ch)
            y = y_ref[pl.ds(r, ch), :]
            y_hi, y_lo = _split_bf16(y)
            mu = _dot(y_hi, avg_ref[...]) + _dot(y_lo, avg_ref[...])
            dlt = y - mu
            d_hi, d_lo = _split_bf16(dlt * dlt)
            var = _dot(d_hi, avg_ref[...]) + _dot(d_lo, avg_ref[...])
            yn = dlt * lax.rsqrt(var + NORM_EPS)
            g = src_ref[0, pl.ds(r, ch), 2 * RET_W + RET_VW:2 * RET_W + 2 * RET_VW]
            out_ref[0, pl.ds(r, ch), :] = _silu(g) * (yn * gn_ref[...])
            return carry
        lax.fori_loop(0, n // ch, body, 0)

    s_zero = jnp.zeros((RET_W, RET_VW), F32)
    stage(rc_ref, n_ctx, False)
    sc_f = scan(rc_ref, yc_s, n_ctx, 0, s_zero, False)
    sc_b = scan(rc_ref, yc_s, n_ctx, 1, s_zero, True)
    if ctx_out:
        finish(rc_ref, yc_s, oc_ref, n_ctx)
    else:
        oc_ref[...] = jnp.zeros(oc_ref.shape, F32)
    stage(rl_ref, n_lat, True)
    scan(rl_ref, yl_s, n_lat, 0, sc_f, False)
    scan(rl_ref, yl_s, n_lat, 1, sc_b, True)
    finish(rl_ref, yl_s, ol_ref, n_lat)


def _retention(r_lat, r_ctx, log_gamma, cos_t, sin_t, gn_w, ctx_out):
    b, n_lat, w = r_lat.shape
    n_ctx = r_ctx.shape[1]
    grp = jnp.arange(RET_VW) // RET_V_DIM
    avg = ((grp[:, None] == grp[None, :]).astype(F32) / RET_V_DIM).astype(BF16)
    tab = pl.BlockSpec((n_lat, RET_W), lambda i, lg: (0, 0))
    grid_spec = pltpu.PrefetchScalarGridSpec(
        num_scalar_prefetch=1,
        grid=(b,),
        in_specs=[pl.BlockSpec((1, n_lat, w), lambda i, lg: (i, 0, 0)),
                  pl.BlockSpec((1, n_ctx, w), lambda i, lg: (i, 0, 0)),
                  tab, tab,
                  pl.BlockSpec((1, RET_VW), lambda i, lg: (0, 0)),
                  pl.BlockSpec((RET_VW, RET_VW), lambda i, lg: (0, 0))],
        out_specs=[pl.BlockSpec((1, n_lat, RET_VW), lambda i, lg: (i, 0, 0)),
                   pl.BlockSpec((1, n_ctx, RET_VW), lambda i, lg: (i, 0, 0))],
        scratch_shapes=[pltpu.VMEM((max(n_lat, n_ctx), RET_W), F32),
                        pltpu.VMEM((max(n_lat, n_ctx), RET_W), F32),
                        pltpu.VMEM((n_lat, RET_VW), F32),
                        pltpu.VMEM((n_ctx, RET_VW), F32),
                        pltpu.VMEM((2, RET_HEADS * RET_CHUNK, RET_CHUNK), F32),
                        pltpu.VMEM((2, RET_CHUNK, RET_W), F32),
                        pltpu.VMEM((2, RET_CHUNK, RET_W), F32),
                        pltpu.VMEM((2, RET_W, RET_VW), F32)],
    )
    return pl.pallas_call(
        functools.partial(_ret_kernel, n_lat=n_lat, n_ctx=n_ctx, ctx_out=ctx_out),
        grid_spec=grid_spec,
        out_shape=[jax.ShapeDtypeStruct((b, n_lat, RET_VW), F32),
                   jax.ShapeDtypeStruct((b, n_ctx, RET_VW), F32)],
        compiler_params=_cp(("parallel",), VMEM_LIMIT),
        name="retention",
    )(log_gamma.reshape(-1), r_lat, r_ctx, cos_t, sin_t, gn_w.reshape(1, RET_VW), avg)


def _mla_proj_kernel(m_ref, qn_ref, kvn_ref, wq_ref, wk_ref, wv_ref, cos_ref, sin_ref,
                     q_ref, k_ref, v_ref, *, rope):
    m = m_ref[0]
    cq = m[:, :MLA_Q_RANK]
    ckv = m[:, MLA_Q_RANK:MLA_Q_RANK + MLA_KV_RANK]
    kr = m[:, MLA_Q_RANK + MLA_KV_RANK:]

    def rms(x, w):
        var = jnp.mean(x * x, axis=-1, keepdims=True)
        return (x * lax.rsqrt(var + NORM_EPS)) * w

    q = _dot(rms(cq, qn_ref[...]).astype(BF16), wq_ref[...])
    ckv_n = rms(ckv, kvn_ref[...]).astype(BF16)
    k = _dot(ckv_n, wk_ref[...])
    v_ref[0] = _dot(ckv_n, wv_ref[...]).astype(BF16)
    if rope:
        cs = cos_ref[...]
        sn = sin_ref[...]
        kr = kr * cs + _rope_partner(kr, HEAD_PAD, MLA_NOPE_DIM) * sn
    scale = (MLA_NOPE_DIM + MLA_ROPE_DIM) ** -0.5
    for h in range(MLA_HEADS):
        sl = slice(h * HEAD_PAD, (h + 1) * HEAD_PAD)
        qh = q[:, sl]
        if rope:
            qh = qh * cs + _rope_partner(qh, HEAD_PAD, MLA_NOPE_DIM) * sn
        q_ref[0, :, sl] = (qh * scale).astype(BF16)
        k_ref[0, :, sl] = (k[:, sl] + kr).astype(BF16)


def _mla_proj(m, qn_w, kvn_w, wq_pad, wk_pad, wv, cos_t, sin_t, rope):
    b, n, w = m.shape
    t = min(512, n)
    hw = MLA_HEADS * HEAD_PAD
    vw = MLA_HEADS * MLA_V_DIM
    const = lambda shape: pl.BlockSpec(shape, lambda i, j: (0, 0))
    tab = pl.BlockSpec((t, HEAD_PAD), lambda i, j: (j, 0))
    return pl.pallas_call(
        functools.partial(_mla_proj_kernel, rope=rope),
        grid=(b, n // t),
        in_specs=[pl.BlockSpec((1, t, w), lambda i, j: (i, j, 0)),
                  const((1, MLA_Q_RANK)), const((1, MLA_KV_RANK)),
                  const((MLA_Q_RANK, hw)), const((MLA_KV_RANK, hw)), const((MLA_KV_RANK, vw)),
                  tab, tab],
        out_specs=[pl.BlockSpec((1, t, hw), lambda i, j: (i, j, 0)),
                   pl.BlockSpec((1, t, hw), lambda i, j: (i, j, 0)),
                   pl.BlockSpec((1, t, vw), lambda i, j: (i, j, 0))],
        out_shape=[jax.ShapeDtypeStruct((b, n, hw), BF16),
                   jax.ShapeDtypeStruct((b, n, hw), BF16),
                   jax.ShapeDtypeStruct((b, n, vw), BF16)],
        compiler_params=_cp(("parallel", "parallel"), VMEM_LIMIT),
        name="mla_proj",
    )(m, qn_w.reshape(1, -1), kvn_w.reshape(1, -1), wq_pad, wk_pad, wv, cos_t, sin_t)


def _attn_kernel(*refs, n_seg):
    q_ref = refs[0]
    k_refs = refs[1:1 + n_seg]
    v_refs = refs[1 + n_seg:1 + 2 * n_seg]
    o_ref = refs[1 + 2 * n_seg]
    tq = q_ref.shape[1]
    lane = lax.broadcasted_iota(jnp.int32, (tq, 2 * MLA_V_DIM), 1)
    for pair in range(MLA_HEADS // 2):
        outs = []
        for h in (2 * pair, 2 * pair + 1):
            sl = slice(h * HEAD_PAD, (h + 1) * HEAD_PAD)
            qh = q_ref[0, :, sl]
            s = [lax.dot_general(qh, kr[0, :, sl], (((1,), (1,)), ((), ())),
                                 preferred_element_type=F32) for kr in k_refs]
            mx = s[0].max(axis=-1, keepdims=True)
            for si in s[1:]:
                mx = jnp.maximum(mx, si.max(axis=-1, keepdims=True))
            den = jnp.zeros((tq, 1), F32)
            acc = jnp.zeros((tq, 2 * MLA_V_DIM), F32)
            for si, vr in zip(s, v_refs):
                p = jnp.exp(si - mx)
                den = den + p.sum(axis=-1, keepdims=True)
                acc = acc + _dot(p.astype(BF16), vr[0, :, pair * 2 * MLA_V_DIM:(pair + 1) * 2 * MLA_V_DIM])
            outs.append(acc * (1.0 / den))
        o_ref[0, :, pair * 2 * MLA_V_DIM:(pair + 1) * 2 * MLA_V_DIM] = jnp.where(
            lane < MLA_V_DIM, outs[0], outs[1])


def _attention(q, ks, vs):
    b, nq, hw = q.shape
    tq = min(256, nq)
    vw = MLA_HEADS * MLA_V_DIM
    n_seg = len(ks)
    seg_spec = lambda a: pl.BlockSpec((1,) + a.shape[1:], lambda i, j: (i, 0, 0))
    return pl.pallas_call(
        functools.partial(_attn_kernel, n_seg=n_seg),
        grid=(b, nq // tq),
        in_specs=[pl.BlockSpec((1, tq, hw), lambda i, j: (i, j, 0))]
                 + [seg_spec(a) for a in ks] + [seg_spec(a) for a in vs],
        out_specs=pl.BlockSpec((1, tq, vw), lambda i, j: (i, j, 0)),
        out_shape=jax.ShapeDtypeStruct((b, nq, vw), F32),
        compiler_params=_cp(("parallel", "parallel"), VMEM_LIMIT),
        name="mla_attention",
    )(q, *ks, *vs)


def _outproj_kernel(cv_ref, rt_ref, ml_ref, x_ref, g1_ref, wo_ref, nw_ref, sh_ref, sc_ref,
                    rwh_ref, rwl_ref, rb_ref, x1_ref, h2_ref, idx_ref, wt_ref):
    c0, c1 = CONV_CH, CONV_CH + RET_VW
    y = (_dot(cv_ref[0].astype(BF16), wo_ref[:c0, :])
         + _dot(rt_ref[0].astype(BF16), wo_ref[c0:c1, :])
         + _dot(ml_ref[0].astype(BF16), wo_ref[c1:, :]))
    x1 = x_ref[0] + g1_ref[0] * y
    x1_ref[0] = x1
    h2 = _rms_mod(x1, nw_ref[...], sh_ref[0], sc_ref[0])
    h2_ref[0] = h2
    h_hi, h_lo = _split_bf16(h2)
    logits = _dot(h_hi, rwh_ref[...]) + _dot(h_lo, rwh_ref[...]) + _dot(h_hi, rwl_ref[...])
    scores = _sigmoid(logits)
    sel = scores + rb_ref[...]
    t = scores.shape[0]
    eio = lax.broadcasted_iota(jnp.int32, (t, N_EXPERTS), 1).astype(F32)
    lane = lax.broadcasted_iota(jnp.int32, (t, LANES), 1)
    idx_out = jnp.zeros((t, LANES), jnp.int32)
    wt_out = jnp.zeros((t, LANES), F32)
    wsum = jnp.zeros((t, 1), F32)
    for k in range(TOP_K):
        mx = jnp.max(sel, axis=-1, keepdims=True)
        ik = jnp.min(jnp.where(sel == mx, eio, float(N_EXPERTS)), axis=-1, keepdims=True)
        hit = eio == ik
        wk = jnp.sum(jnp.where(hit, scores, 0.0), axis=-1, keepdims=True)
        sel = jnp.where(hit, -jnp.inf, sel)
        idx_out = jnp.where(lane == k, ik.astype(jnp.int32), idx_out)
        wt_out = jnp.where(lane == k, wk, wt_out)
        wsum = wsum + wk
    idx_ref[0] = idx_out
    wt_ref[0] = wt_out / wsum * ROUTED_SCALE


def _outproj(conv, ret, mla, x, g1, wo, nw, sh, sc, rw_hi, rw_lo, rb):
    b, n, d = x.shape
    t = min(256, n)
    tok = lambda w: pl.BlockSpec((1, t, w), lambda i, j: (i, j, 0))
    per_b = pl.BlockSpec((1, 1, d), lambda i, j: (i, 0, 0))
    const = lambda shape: pl.BlockSpec(shape, lambda i, j: (0, 0))
    return pl.pallas_call(
        _outproj_kernel,
        grid=(b, n // t),
        in_specs=[tok(CONV_CH), tok(RET_VW), tok(MLA_HEADS * MLA_V_DIM), tok(d), per_b,
                  const((d, d)), const((1, d)), per_b, per_b,
                  const((d, N_EXPERTS)), const((d, N_EXPERTS)), const((1, N_EXPERTS))],
        out_specs=[tok(d), tok(d), tok(LANES), tok(LANES)],
        out_shape=[jax.ShapeDtypeStruct((b, n, d), F32),
                   jax.ShapeDtypeStruct((b, n, d), F32),
                   jax.ShapeDtypeStruct((b, n, LANES), jnp.int32),
                   jax.ShapeDtypeStruct((b, n, LANES), F32)],
        compiler_params=_cp(("parallel", "parallel"), VMEM_LIMIT),
        name="outproj_norm2_router",
    )(conv, ret, mla, x, g1, wo, nw.reshape(1, d), sh, sc, rw_hi, rw_lo, rb.reshape(1, N_EXPERTS))


def _moe_kernel(be_ref, nb_ref, tok_ref, h_ref, w_ref, wg_ref, wu_ref, wd_ref, o_ref, xt_ref, yt_ref,
                *, blocks_per_tile):
    b = pl.program_id(0)
    tile = b // blocks_per_tile
    local = b % blocks_per_tile

    @pl.when(local == 0)
    def _():
        o_ref[...] = jnp.zeros(o_ref.shape, F32)

    @pl.when(local < nb_ref[tile])
    def _():
        for mi in range(MOE_BLOCK):
            t8 = pl.multiple_of(tok_ref[b, mi], SUBLANES)
            xt_ref[pl.ds(mi, SUBLANES, stride=MOE_STRIDE), :] = h_ref[0, pl.ds(t8, SUBLANES), :]
        x = jnp.concatenate([xt_ref[pl.ds(j * MOE_STRIDE, MOE_BLOCK), :] for j in range(SUBLANES)],
                            axis=1).astype(BF16)
        g = _dot(x, wg_ref[0])
        u = _dot(x, wu_ref[0])
        a = (_silu(g) * u) * w_ref[0]
        y = _dot(a.astype(BF16), wd_ref[0])
        for j in range(SUBLANES):
            yt_ref[pl.ds(j * MOE_STRIDE, MOE_BLOCK), :] = y[:, j * LANES:(j + 1) * LANES]
        for mi in range(MOE_BLOCK):
            t8 = pl.multiple_of(tok_ref[b, mi], SUBLANES)
            o_ref[0, pl.ds(t8, SUBLANES), :] = (o_ref[0, pl.ds(t8, SUBLANES), :]
                                                + yt_ref[pl.ds(mi, SUBLANES, stride=MOE_STRIDE), :])


def _moe_tile_size(n_tok):
    for s in (2048, 1024, 512, 256, 128):
        if n_tok % s == 0:
            return s
    raise ValueError(f"token count {n_tok} must be a multiple of 128")


def _routed_experts(h2, idx, wts, wg, wu, wd):
    n_tok, d = h2.shape
    ts = _moe_tile_size(n_tok)
    n_tiles = n_tok // ts
    n_assign = ts * TOP_K
    bpt = n_assign // MOE_BLOCK + N_EXPERTS
    n_slots = bpt * MOE_BLOCK

    def tile_lists(e_flat, w_flat):
        order = jnp.argsort(e_flat)
        sorted_e = e_flat[order]
        counts = jnp.bincount(e_flat, length=N_EXPERTS)
        padded = (counts + MOE_BLOCK - 1) // MOE_BLOCK * MOE_BLOCK
        pad_end = jnp.cumsum(padded)
        pad_start = pad_end - padded
        start = jnp.cumsum(counts) - counts
        dest = pad_start[sorted_e] + jnp.arange(n_assign, dtype=jnp.int32) - start[sorted_e]
        tok = (order // TOP_K).astype(jnp.int32)
        slot_tok = jnp.zeros((n_slots,), jnp.int32).at[dest].set(tok * SUBLANES)
        slot_w = jnp.zeros((n_slots,), F32).at[dest].set(w_flat[order])
        blk = jnp.arange(bpt, dtype=jnp.int32) * MOE_BLOCK
        block_e = jnp.minimum(jnp.searchsorted(pad_end, blk, side='right'), N_EXPERTS - 1)
        return slot_tok, slot_w, block_e.astype(jnp.int32), (pad_end[-1] // MOE_BLOCK).astype(jnp.int32)

    slot_tok, slot_w, block_e, n_used = jax.vmap(tile_lists)(
        idx.reshape(n_tiles, n_assign).astype(jnp.int32), wts.reshape(n_tiles, n_assign))

    group = max(g for g in range(1, n_tiles + 1) if n_tiles % g == 0 and g * n_slots * 4 <= 512 * 1024)
    h3 = h2.reshape(n_tiles, ts * SUBLANES, LANES)
    outs = []
    for c in range(n_tiles // group):
        t0 = c * group
        nb = group * bpt
        grid_spec = pltpu.PrefetchScalarGridSpec(
            num_scalar_prefetch=3,
            grid=(nb,),
            in_specs=[pl.BlockSpec((1, ts * SUBLANES, LANES), lambda i, be, nu, tk: (t0 + i // bpt, 0, 0)),
                      pl.BlockSpec((1, MOE_BLOCK, 1), lambda i, be, nu, tk: (i, 0, 0)),
                      pl.BlockSpec((1, d, EXPERT_DIM), lambda i, be, nu, tk: (be[i], 0, 0)),
                      pl.BlockSpec((1, d, EXPERT_DIM), lambda i, be, nu, tk: (be[i], 0, 0)),
                      pl.BlockSpec((1, EXPERT_DIM, d), lambda i, be, nu, tk: (be[i], 0, 0))],
            out_specs=pl.BlockSpec((1, ts * SUBLANES, LANES), lambda i, be, nu, tk: (i // bpt, 0, 0)),
            scratch_shapes=[pltpu.VMEM((SUBLANES * MOE_STRIDE, LANES), F32),
                            pltpu.VMEM((SUBLANES * MOE_STRIDE, LANES), F32)],
        )
        outs.append(pl.pallas_call(
            functools.partial(_moe_kernel, blocks_per_tile=bpt),
            grid_spec=grid_spec,
            out_shape=jax.ShapeDtypeStruct((group, ts * SUBLANES, LANES), F32),
            compiler_params=_cp(("arbitrary",), VMEM_LIMIT),
            name="routed_experts",
        )(block_e[t0:t0 + group].reshape(-1), n_used[t0:t0 + group],
          slot_tok[t0:t0 + group].reshape(nb, MOE_BLOCK), h3,
          slot_w[t0:t0 + group].reshape(nb, MOE_BLOCK, 1), wg, wu, wd))
    return jnp.concatenate(outs, axis=0).reshape(n_tok, d)


def _ffn_out_kernel(x_ref, h_ref, r_ref, g2_ref, sg_ref, su_ref, sd_ref, fw_ref, o_ref, *, final):
    h = h_ref[...].astype(BF16)
    a = _silu(_dot(h, sg_ref[...])) * _dot(h, su_ref[...])
    y = r_ref[...] + _dot(a.astype(BF16), sd_ref[...])
    x2 = x_ref[...] + g2_ref[0] * y
    if final:
        var = jnp.mean(x2 * x2, axis=-1, keepdims=True)
        x2 = (x2 * lax.rsqrt(var + NORM_EPS)) * fw_ref[...]
    o_ref[...] = x2


def _ffn_out(x1, h2, routed, g2_rows, tiles_per_row, sg, su, sd, fw, final):
    n_tok, d = x1.shape
    t = 256
    last = g2_rows.shape[0] - 1
    tok = pl.BlockSpec((t, d), lambda i: (i, 0))
    const = lambda shape: pl.BlockSpec(shape, lambda i: (0, 0))
    return pl.pallas_call(
        functools.partial(_ffn_out_kernel, final=final),
        grid=(n_tok // t,),
        in_specs=[tok, tok, tok,
                  pl.BlockSpec((1, 1, d), lambda i: (jnp.minimum(i // tiles_per_row, last), 0, 0)),
                  const((d, EXPERT_DIM)), const((d, EXPERT_DIM)), const((EXPERT_DIM, d)), const((1, d))],
        out_specs=tok,
        out_shape=jax.ShapeDtypeStruct((n_tok, d), F32),
        compiler_params=_cp(("parallel",), VMEM_LIMIT),
        name="shared_expert_residual",
    )(x1, h2, routed, g2_rows, sg, su, sd, fw.reshape(1, d))


def _rope_tables(rows, dim, group, lo):
    pos_r = jnp.repeat(jnp.arange(rows, dtype=F32), GRID_W)
    pos_c = jnp.tile(jnp.arange(GRID_W, dtype=F32), rows)
    n_freq = dim // 4
    inv = ROPE_BASE ** (-jnp.arange(n_freq, dtype=F32) / n_freq)
    ang = jnp.concatenate([pos_r[:, None] * inv, pos_c[:, None] * inv], axis=-1)
    cos, sin = jnp.cos(ang), jnp.sin(ang)
    n = rows * GRID_W
    half = dim // 2
    cos_g = jnp.ones((n, group), F32).at[:, lo:lo + dim].set(jnp.concatenate([cos, cos], axis=-1))
    sin_g = jnp.zeros((n, group), F32).at[:, lo:lo + dim].set(jnp.concatenate([-sin, sin], axis=-1))
    reps = LANES // group
    return jnp.tile(cos_g, (1, reps)), jnp.tile(sin_g, (1, reps))


def _pad_in_proj(w_in):
    d = w_in.shape[0]
    body = w_in[:, :IN_COLS_PAD - HEAD_PAD]
    kr = w_in[:, IN_COLS_PAD - HEAD_PAD:]
    kr_pad = jnp.zeros((d, HEAD_PAD), w_in.dtype).at[:, MLA_NOPE_DIM:MLA_NOPE_DIM + MLA_ROPE_DIM].set(kr)
    return jnp.concatenate([body, kr_pad], axis=1).astype(BF16)


def _pad_heads(w, width):
    k = w.shape[0]
    w3 = w.reshape(k, MLA_HEADS, width)
    return jnp.zeros((k, MLA_HEADS, HEAD_PAD), w.dtype).at[:, :, :width].set(w3).reshape(
        k, MLA_HEADS * HEAD_PAD).astype(BF16)


def kernel(x, c, ctx, c_ctx, mod_w, mod_b, norm1_w, w_in, conv_w, conv_b, conv_ln_w, conv_ln_b,
           ret_decay_logit, ret_gn_w, q_norm_w, w_uq, kv_norm_w, w_ukv, w_out, norm2_w,
           router_w, router_b, exp_w_gate, exp_w_up, exp_w_down, sh_w_gate, sh_w_up, sh_w_down,
           final_norm_w):
    b, n_lat, d = x.shape
    n_ctx = ctx.shape[1]
    depth = mod_w.shape[0]
    rows = n_lat // GRID_W
    cos_ret, sin_ret = _rope_tables(rows, RET_QK_DIM, RET_QK_DIM, 0)
    cos_mla, sin_mla = _rope_tables(rows, MLA_ROPE_DIM, HEAD_PAD, MLA_NOPE_DIM)

    mod_rows = -(-(b + 1) // SUBLANES) * SUBLANES
    cc = jnp.zeros((mod_rows, d), F32).at[:b].set(c).at[b].set(c_ctx)

    xl, xc = x, ctx
    for i in range(depth):
        last = i == depth - 1
        mod = _modulation(cc, mod_w[i], mod_b[i])
        ml = mod[:b].reshape(b, 1, 6, d)
        sh1, sc1, g1, sh2, sc2, g2 = [ml[:, :, j, :] for j in range(6)]
        mc = jnp.broadcast_to(mod[b].reshape(1, 1, 6, d), (b, 1, 6, d))
        csh1, csc1, cg1, csh2, csc2, cg2 = [mc[:, :, j, :] for j in range(6)]

        w_in_p = _pad_in_proj(w_in[i])
        ul, rl, mlat = _inproj(xl, norm1_w[i], sh1, sc1, w_in_p)
        uc, rc, mctx = _inproj(xc, norm1_w[i], csh1, csc1, w_in_p)

        conv_l = _conv(ul, conv_w[i], conv_b[i], conv_ln_w[i], conv_ln_b[i])
        log_gamma = jax.nn.log_sigmoid(ret_decay_logit[i].astype(F32))
        ret_l, ret_c = _retention(rl, rc, log_gamma, cos_ret, sin_ret, ret_gn_w[i], not last)

        wq_p = _pad_heads(w_uq[i], MLA_NOPE_DIM + MLA_ROPE_DIM)
        wkv = w_ukv[i].reshape(MLA_KV_RANK, MLA_HEADS, MLA_NOPE_DIM + MLA_V_DIM)
        wk_p = _pad_heads(wkv[:, :, :MLA_NOPE_DIM].reshape(MLA_KV_RANK, -1), MLA_NOPE_DIM)
        wv = wkv[:, :, MLA_NOPE_DIM:].reshape(MLA_KV_RANK, -1).astype(BF16)
        ql, kl, vl = _mla_proj(mlat, q_norm_w[i], kv_norm_w[i], wq_p, wk_p, wv, cos_mla, sin_mla, True)
        qc, kc, vc = _mla_proj(mctx, q_norm_w[i], kv_norm_w[i], wq_p, wk_p, wv,
                               cos_mla[:n_ctx], sin_mla[:n_ctx], False)
        mla_l = _attention(ql, [kc, kl], [vc, vl])

        wo = w_out[i].astype(BF16)
        rw_hi = router_w[i].astype(BF16)
        rw_lo = (router_w[i] - rw_hi.astype(F32)).astype(BF16)
        x1l, h2l, idx_l, wt_l = _outproj(conv_l, ret_l, mla_l, xl, g1, wo, norm2_w[i], sh2, sc2,
                                         rw_hi, rw_lo, router_b[i])
        n_l = b * n_lat
        if last:
            x1 = x1l.reshape(n_l, d)
            h2 = h2l.reshape(n_l, d)
            idx = idx_l.reshape(n_l, LANES)[:, :TOP_K]
            wts = wt_l.reshape(n_l, LANES)[:, :TOP_K]
            g2_rows = g2
        else:
            conv_c = _conv(uc, conv_w[i], conv_b[i], conv_ln_w[i], conv_ln_b[i])
            mla_c = _attention(qc, [kc], [vc])
            x1c, h2c, idx_c, wt_c = _outproj(conv_c, ret_c, mla_c, xc, cg1, wo, norm2_w[i], csh2, csc2,
                                             rw_hi, rw_lo, router_b[i])
            n_c = b * n_ctx
            x1 = jnp.concatenate([x1l.reshape(n_l, d), x1c.reshape(n_c, d)], axis=0)
            h2 = jnp.concatenate([h2l.reshape(n_l, d), h2c.reshape(n_c, d)], axis=0)
            idx = jnp.concatenate([idx_l.reshape(n_l, LANES), idx_c.reshape(n_c, LANES)], axis=0)[:, :TOP_K]
            wts = jnp.concatenate([wt_l.reshape(n_l, LANES), wt_c.reshape(n_c, LANES)], axis=0)[:, :TOP_K]
            g2_rows = jnp.concatenate([g2, cg2[:1]], axis=0)

        routed = _routed_experts(h2, idx, wts, exp_w_gate[i].astype(BF16), exp_w_up[i].astype(BF16),
                                 exp_w_down[i].astype(BF16))
        x2 = _ffn_out(x1, h2, routed, g2_rows, n_lat // 256, sh_w_gate[i].astype(BF16),
                      sh_w_up[i].astype(BF16), sh_w_down[i].astype(BF16), final_norm_w, last)
        xl = x2[:n_l].reshape(b, n_lat, d)
        if not last:
            xc = x2[n_l:].reshape(b, n_ctx, d)
    return xl
```

```python
import functools

import jax
import jax.numpy as jnp
from jax import lax
from jax.experimental import pallas as pl
from jax.experimental.pallas import tpu as pltpu

F32 = jnp.float32
BF16 = jnp.bfloat16

D_MODEL = 1024
GRID_W = 64
CONV_CH = 256
CONV_K = 31
RET_HEADS = 4
RET_QK_DIM = 32
RET_V_DIM = 64
RET_CHUNK = 128
MLA_HEADS = 8
MLA_NOPE_DIM = 64
MLA_ROPE_DIM = 32
MLA_V_DIM = 64
MLA_Q_RANK = 256
MLA_KV_RANK = 128
ROPE_BASE = 10000.0
N_EXPERTS = 64
TOP_K = 6
EXPERT_DIM = 256
ROUTED_SCALE = 2.5
NORM_EPS = 1e-6

LANES = 128
SUBLANES = 8
HEAD_PAD = 128
RET_W = RET_HEADS * RET_QK_DIM
RET_VW = RET_HEADS * RET_V_DIM
IN_COLS_PAD = 2 * CONV_CH + 2 * RET_W + 2 * RET_VW + MLA_Q_RANK + MLA_KV_RANK + HEAD_PAD
MOE_BLOCK = 128
MOE_STRIDE = MOE_BLOCK + SUBLANES
VMEM_LIMIT = 56 * 1024 * 1024


def _cp(sem, vmem=None):
    return pltpu.CompilerParams(dimension_semantics=sem, vmem_limit_bytes=vmem)


def _dot(a, b):
    return jnp.dot(a, b, preferred_element_type=F32)


def _split_bf16(a):
    hi = a.astype(BF16)
    lo = (a - hi.astype(F32)).astype(BF16)
    return hi, lo


def _sigmoid(x):
    return 1.0 / (1.0 + jnp.exp(-x))


def _silu(x):
    return x * _sigmoid(x)


def _mod_kernel(c_ref, w_ref, b_ref, o_ref):
    a_hi, a_lo = _split_bf16(_silu(c_ref[...]))
    w_hi, w_lo = _split_bf16(w_ref[...])
    o_ref[...] = _dot(a_hi, w_hi) + _dot(a_lo, w_hi) + _dot(a_hi, w_lo) + b_ref[...]


def _modulation(cc, w, b):
    rows, d = cc.shape
    n = w.shape[1]
    bn = 1536
    return pl.pallas_call(
        _mod_kernel,
        grid=(n // bn,),
        in_specs=[pl.BlockSpec((rows, d), lambda j: (0, 0)),
                  pl.BlockSpec((d, bn), lambda j: (0, j)),
                  pl.BlockSpec((1, bn), lambda j: (0, j))],
        out_specs=pl.BlockSpec((rows, bn), lambda j: (0, j)),
        out_shape=jax.ShapeDtypeStruct((rows, n), F32),
        compiler_params=_cp(("arbitrary",), VMEM_LIMIT),
        name="modulation",
    )(cc, w, b.reshape(1, n))


def _rms_mod(x, nw, sh, sc):
    var = jnp.mean(x * x, axis=-1, keepdims=True)
    h = (x * lax.rsqrt(var + NORM_EPS)) * nw
    return h * (1.0 + sc) + sh


def _inproj_kernel(x_ref, nw_ref, sh_ref, sc_ref, w_ref, u_ref, r_ref, m_ref):
    h = _rms_mod(x_ref[0], nw_ref[...], sh_ref[0], sc_ref[0])
    z = _dot(h.astype(BF16), w_ref[...])
    c0 = 2 * CONV_CH
    c1 = c0 + 2 * RET_W + 2 * RET_VW
    u_ref[0] = z[:, :c0]
    r_ref[0] = z[:, c0:c1]
    m_ref[0] = z[:, c1:]


def _inproj(x, nw, sh, sc, w_pad):
    b, n, d = x.shape
    t = min(512, n)
    wu, wr, wm = 2 * CONV_CH, 2 * RET_W + 2 * RET_VW, MLA_Q_RANK + MLA_KV_RANK + HEAD_PAD
    tok = lambda w: pl.BlockSpec((1, t, w), lambda i, j: (i, j, 0))
    per_b = pl.BlockSpec((1, 1, d), lambda i, j: (i, 0, 0))
    return pl.pallas_call(
        _inproj_kernel,
        grid=(b, n // t),
        in_specs=[tok(d), pl.BlockSpec((1, d), lambda i, j: (0, 0)), per_b, per_b,
                  pl.BlockSpec((d, IN_COLS_PAD), lambda i, j: (0, 0))],
        out_specs=[tok(wu), tok(wr), tok(wm)],
        out_shape=[jax.ShapeDtypeStruct((b, n, wu), F32),
                   jax.ShapeDtypeStruct((b, n, wr), F32),
                   jax.ShapeDtypeStruct((b, n, wm), F32)],
        compiler_params=_cp(("parallel", "parallel"), VMEM_LIMIT),
        name="norm1_inproj",
    )(x, nw.reshape(1, d), sh, sc, w_pad)


_CONV_PAD = 16
_CONV_ROWS = 128


def _conv_kernel(u_ref, cw_ref, cb_ref, lw_ref, lb_ref, o_ref, hp_ref, *, n):
    c = CONV_CH
    hp_ref[0:_CONV_PAD, :] = jnp.zeros((_CONV_PAD, c), F32)
    hp_ref[n + _CONV_PAD:n + 2 * _CONV_PAD, :] = jnp.zeros((_CONV_PAD, c), F32)

    def glu(i, carry):
        r = pl.multiple_of(i * _CONV_ROWS, _CONV_ROWS)
        u = u_ref[0, pl.ds(r, _CONV_ROWS), :]
        hp_ref[pl.ds(r + _CONV_PAD, _CONV_ROWS), :] = u[:, :c] * _sigmoid(u[:, c:])
        return carry

    lax.fori_loop(0, n // _CONV_ROWS, glu, 0)

    def conv(i, carry):
        r = pl.multiple_of(i * _CONV_ROWS, _CONV_ROWS)
        acc = jnp.zeros((_CONV_ROWS, c), F32)
        base = _CONV_PAD - CONV_K // 2
        for q in range((base + CONV_K - 1) // SUBLANES + 1):
            win = hp_ref[pl.ds(r + q * SUBLANES, _CONV_ROWS + SUBLANES), :]
            for s in range(SUBLANES):
                k = q * SUBLANES + s - base
                if 0 <= k < CONV_K:
                    acc = acc + cw_ref[k:k + 1, :] * win[s:s + _CONV_ROWS, :]
        hcv = acc + cb_ref[...]
        mu = jnp.mean(hcv, axis=-1, keepdims=True)
        dlt = hcv - mu
        var = jnp.mean(dlt * dlt, axis=-1, keepdims=True)
        y = (dlt * lax.rsqrt(var + NORM_EPS)) * lw_ref[...] + lb_ref[...]
        o_ref[0, pl.ds(r, _CONV_ROWS), :] = _silu(y)
        return carry

    lax.fori_loop(0, n // _CONV_ROWS, conv, 0)


def _conv(u, cw, cb, lw, lb):
    b, n, _ = u.shape
    c = CONV_CH
    vec = pl.BlockSpec((1, c), lambda i: (0, 0))
    return pl.pallas_call(
        functools.partial(_conv_kernel, n=n),
        grid=(b,),
        in_specs=[pl.BlockSpec((1, n, 2 * c), lambda i: (i, 0, 0)),
                  pl.BlockSpec((CONV_K, c), lambda i: (0, 0)), vec, vec, vec],
        out_specs=pl.BlockSpec((1, n, c), lambda i: (i, 0, 0)),
        out_shape=jax.ShapeDtypeStruct((b, n, c), F32),
        scratch_shapes=[pltpu.VMEM((n + 2 * _CONV_PAD, c), F32)],
        compiler_params=_cp(("parallel",), VMEM_LIMIT),
        name="conformer_conv",
    )(u, cw, cb.reshape(1, c), lw.reshape(1, c), lb.reshape(1, c))


def _rope_partner(x, group, lo):
    half = 16
    lane = lax.broadcasted_iota(jnp.int32, x.shape, 1) % group
    first = (lane >= lo) & (lane < lo + half)
    return jnp.where(first, pltpu.roll(x, LANES - half, 1), pltpu.roll(x, half, 1))


def _ret_kernel(lg_ref, rl_ref, rc_ref, cos_ref, sin_ref, gn_ref, avg_ref, ol_ref, oc_ref,
                q_s, k_s, yl_s, yc_s, dst_s, dq_s, dk_s, dch_s, *, n_lat, n_ctx, ctx_out):
    ch = RET_CHUNK
    lane_q = lax.broadcasted_iota(jnp.int32, (1, RET_W), 1) // RET_QK_DIM
    lane_v = lax.broadcasted_iota(jnp.int32, (1, RET_VW), 1) // RET_V_DIM
    row_h = lax.broadcasted_iota(jnp.int32, (RET_W, 1), 0) // RET_QK_DIM
    bd = (row_h == lane_v).astype(F32)
    qmask = [(lane_q == h).astype(F32) for h in range(RET_HEADS)]
    vmask = [(lane_v == h).astype(F32) for h in range(RET_HEADS)]

    ri = lax.broadcasted_iota(jnp.int32, (ch, ch), 0).astype(F32)
    ci = lax.broadcasted_iota(jnp.int32, (ch, ch), 1).astype(F32)
    rowi = lax.broadcasted_iota(jnp.int32, (ch, 1), 0).astype(F32)
    for d in range(2):
        lg_lane = jnp.zeros((1, RET_W), F32)
        lg_row = jnp.zeros((RET_W, 1), F32)
        for h in range(RET_HEADS):
            lg = lg_ref[d * RET_HEADS + h]
            lg_lane = jnp.where(lane_q == h, lg, lg_lane)
            lg_row = jnp.where(row_h == h, lg, lg_row)
            rel = (ri - ci) if d == 0 else (ci - ri)
            dst_s[d, h * ch:(h + 1) * ch, :] = jnp.where(
                rel >= 0, jnp.exp(lg * jnp.maximum(rel, 0.0)), 0.0)
        if d == 0:
            dq_s[d] = jnp.exp(lg_lane * (rowi + 1.0))
            dk_s[d] = jnp.exp(lg_lane * (ch - 1.0 - rowi))
        else:
            dq_s[d] = jnp.exp(lg_lane * (ch - rowi))
            dk_s[d] = jnp.exp(lg_lane * rowi)
        dch_s[d] = jnp.exp(lg_row * float(ch)) * jnp.ones((1, RET_VW), F32)

    kscale = RET_QK_DIM ** -0.5

    def stage(src_ref, n, rope):
        def body(i, carry):
            r = pl.multiple_of(i * ch, ch)
            q = src_ref[0, pl.ds(r, ch), 0:RET_W]
            k = src_ref[0, pl.ds(r, ch), RET_W:2 * RET_W] * kscale
            if rope:
                cs = cos_ref[pl.ds(r, ch), :]
                sn = sin_ref[pl.ds(r, ch), :]
                q = q * cs + _rope_partner(q, RET_QK_DIM, 0) * sn
                k = k * cs + _rope_partner(k, RET_QK_DIM, 0) * sn
            q_s[pl.ds(r, ch), :] = q
            k_s[pl.ds(r, ch), :] = k
            return carry
        lax.fori_loop(0, n // ch, body, 0)

    def scan(src_ref, y_ref, n, d, s0, accumulate):
        nchunks = n // ch

        def body(i, s):
            c = i if d == 0 else nchunks - 1 - i
            r = pl.multiple_of(c * ch, ch)
            qc = q_s[pl.ds(r, ch), :]
            kc = k_s[pl.ds(r, ch), :]
            vc = src_ref[0, pl.ds(r, ch), 2 * RET_W:2 * RET_W + RET_VW].astype(BF16)
            qst = jnp.concatenate([qc * qmask[h] for h in range(RET_HEADS)], axis=0).astype(BF16)
            inner = lax.dot_general(qst, kc.astype(BF16), (((1,), (1,)), ((), ())),
                                    preferred_element_type=F32) * dst_s[d]
            o = _dot(inner.astype(BF16), vc)
            y = _dot((qc * dq_s[d]).astype(BF16), s.astype(BF16))
            for h in range(RET_HEADS):
                y = y + o[h * ch:(h + 1) * ch, :] * vmask[h]
            kd_t = (kc * dk_s[d]).T.astype(BF16)
            s_new = s * dch_s[d] + _dot(kd_t, vc) * bd
            if accumulate:
                y_ref[pl.ds(r, ch), :] = y_ref[pl.ds(r, ch), :] + y
            else:
                y_ref[pl.ds(r, ch), :] = y
            return s_new

        return lax.fori_loop(0, nchunks, body, s0)

    def finish(src_ref, y_ref, out_ref, n):
        def body(i, carry):
            r = pl.multiple_of(i * ch, ch)
            y = y_ref[pl.ds(r, ch), :]
            y_hi, y_lo = _split_bf16(y)
            mu = _dot(y_hi, avg_ref[...]) + _dot(y_lo, avg_ref[...])
            dlt = y - mu
            d_hi, d_lo = _split_bf16(dlt * dlt)
            var = _dot(d_hi, avg_ref[...]) + _dot(d_lo, avg_ref[...])
            yn = dlt * lax.rsqrt(var + NORM_EPS)
            g = src_ref[0, pl.ds(r, ch), 2 * RET_W + RET_VW:2 * RET_W + 2 * RET_VW]
            out_ref[0, pl.ds(r, ch), :] = _silu(g) * (yn * gn_ref[...])
            return carry
        lax.fori_loop(0, n // ch, body, 0)

    s_zero = jnp.zeros((RET_W, RET_VW), F32)
    stage(rc_ref, n_ctx, False)
    sc_f = scan(rc_ref, yc_s, n_ctx, 0, s_zero, False)
    sc_b = scan(rc_ref, yc_s, n_ctx, 1, s_zero, True)
    if ctx_out:
        finish(rc_ref, yc_s, oc_ref, n_ctx)
    else:
        oc_ref[...] = jnp.zeros(oc_ref.shape, F32)
    stage(rl_ref, n_lat, True)
    scan(rl_ref, yl_s, n_lat, 0, sc_f, False)
    scan(rl_ref, yl_s, n_lat, 1, sc_b, True)
    finish(rl_ref, yl_s, ol_ref, n_lat)


def _retention(r_lat, r_ctx, log_gamma, cos_t, sin_t, gn_w, ctx_out):
    b, n_lat, w = r_lat.shape
    n_ctx = r_ctx.shape[1]
    grp = jnp.arange(RET_VW) // RET_V_DIM
    avg = ((grp[:, None] == grp[None, :]).astype(F32) / RET_V_DIM).astype(BF16)
    tab = pl.BlockSpec((n_lat, RET_W), lambda i, lg: (0, 0))
    grid_spec = pltpu.PrefetchScalarGridSpec(
        num_scalar_prefetch=1,
        grid=(b,),
        in_specs=[pl.BlockSpec((1, n_lat, w), lambda i, lg: (i, 0, 0)),
                  pl.BlockSpec((1, n_ctx, w), lambda i, lg: (i, 0, 0)),
                  tab, tab,
                  pl.BlockSpec((1, RET_VW), lambda i, lg: (0, 0)),
                  pl.BlockSpec((RET_VW, RET_VW), lambda i, lg: (0, 0))],
        out_specs=[pl.BlockSpec((1, n_lat, RET_VW), lambda i, lg: (i, 0, 0)),
                   pl.BlockSpec((1, n_ctx, RET_VW), lambda i, lg: (i, 0, 0))],
        scratch_shapes=[pltpu.VMEM((max(n_lat, n_ctx), RET_W), F32),
                        pltpu.VMEM((max(n_lat, n_ctx), RET_W), F32),
                        pltpu.VMEM((n_lat, RET_VW), F32),
                        pltpu.VMEM((n_ctx, RET_VW), F32),
                        pltpu.VMEM((2, RET_HEADS * RET_CHUNK, RET_CHUNK), F32),
                        pltpu.VMEM((2, RET_CHUNK, RET_W), F32),
                        pltpu.VMEM((2, RET_CHUNK, RET_W), F32),
                        pltpu.VMEM((2, RET_W, RET_VW), F32)],
    )
    return pl.pallas_call(
        functools.partial(_ret_kernel, n_lat=n_lat, n_ctx=n_ctx, ctx_out=ctx_out),
        grid_spec=grid_spec,
        out_shape=[jax.ShapeDtypeStruct((b, n_lat, RET_VW), F32),
                   jax.ShapeDtypeStruct((b, n_ctx, RET_VW), F32)],
        compiler_params=_cp(("parallel",), VMEM_LIMIT),
        name="retention",
    )(log_gamma.reshape(-1), r_lat, r_ctx, cos_t, sin_t, gn_w.reshape(1, RET_VW), avg)


def _mla_proj_kernel(m_ref, qn_ref, kvn_ref, wq_ref, wk_ref, wv_ref, cos_ref, sin_ref,
                     q_ref, k_ref, v_ref, *, rope):
    m = m_ref[0]
    cq = m[:, :MLA_Q_RANK]
    ckv = m[:, MLA_Q_RANK:MLA_Q_RANK + MLA_KV_RANK]
    kr = m[:, MLA_Q_RANK + MLA_KV_RANK:]

    def rms(x, w):
        var = jnp.mean(x * x, axis=-1, keepdims=True)
        return (x * lax.rsqrt(var + NORM_EPS)) * w

    q = _dot(rms(cq, qn_ref[...]).astype(BF16), wq_ref[...])
    ckv_n = rms(ckv, kvn_ref[...]).astype(BF16)
    k = _dot(ckv_n, wk_ref[...])
    v_ref[0] = _dot(ckv_n, wv_ref[...]).astype(BF16)
    if rope:
        cs = cos_ref[...]
        sn = sin_ref[...]
        kr = kr * cs + _rope_partner(kr, HEAD_PAD, MLA_NOPE_DIM) * sn
    scale = (MLA_NOPE_DIM + MLA_ROPE_DIM) ** -0.5
    for h in range(MLA_HEADS):
        sl = slice(h * HEAD_PAD, (h + 1) * HEAD_PAD)
        qh = q[:, sl]
        if rope:
            qh = qh * cs + _rope_partner(qh, HEAD_PAD, MLA_NOPE_DIM) * sn
        q_ref[0, :, sl] = (qh * scale).astype(BF16)
        k_ref[0, :, sl] = (k[:, sl] + kr).astype(BF16)


def _mla_proj(m, qn_w, kvn_w, wq_pad, wk_pad, wv, cos_t, sin_t, rope):
    b, n, w = m.shape
    t = min(512, n)
    hw = MLA_HEADS * HEAD_PAD
    vw = MLA_HEADS * MLA_V_DIM
    const = lambda shape: pl.BlockSpec(shape, lambda i, j: (0, 0))
    tab = pl.BlockSpec((t, HEAD_PAD), lambda i, j: (j, 0))
    return pl.pallas_call(
        functools.partial(_mla_proj_kernel, rope=rope),
        grid=(b, n // t),
        in_specs=[pl.BlockSpec((1, t, w), lambda i, j: (i, j, 0)),
                  const((1, MLA_Q_RANK)), const((1, MLA_KV_RANK)),
                  const((MLA_Q_RANK, hw)), const((MLA_KV_RANK, hw)), const((MLA_KV_RANK, vw)),
                  tab, tab],
        out_specs=[pl.BlockSpec((1, t, hw), lambda i, j: (i, j, 0)),
                   pl.BlockSpec((1, t, hw), lambda i, j: (i, j, 0)),
                   pl.BlockSpec((1, t, vw), lambda i, j: (i, j, 0))],
        out_shape=[jax.ShapeDtypeStruct((b, n, hw), BF16),
                   jax.ShapeDtypeStruct((b, n, hw), BF16),
                   jax.ShapeDtypeStruct((b, n, vw), BF16)],
        compiler_params=_cp(("parallel", "parallel"), VMEM_LIMIT),
        name="mla_proj",
    )(m, qn_w.reshape(1, -1), kvn_w.reshape(1, -1), wq_pad, wk_pad, wv, cos_t, sin_t)


def _attn_kernel(*refs, n_seg):
    q_ref = refs[0]
    k_refs = refs[1:1 + n_seg]
    v_refs = refs[1 + n_seg:1 + 2 * n_seg]
    o_ref = refs[1 + 2 * n_seg]
    tq = q_ref.shape[1]
    lane = lax.broadcasted_iota(jnp.int32, (tq, 2 * MLA_V_DIM), 1)
    for pair in range(MLA_HEADS // 2):
        outs = []
        for h in (2 * pair, 2 * pair + 1):
            sl = slice(h * HEAD_PAD, (h + 1) * HEAD_PAD)
            qh = q_ref[0, :, sl]
            s = [lax.dot_general(qh, kr[0, :, sl], (((1,), (1,)), ((), ())),
                                 preferred_element_type=F32) for kr in k_refs]
            mx = s[0].max(axis=-1, keepdims=True)
            for si in s[1:]:
                mx = jnp.maximum(mx, si.max(axis=-1, keepdims=True))
            den = jnp.zeros((tq, 1), F32)
            acc = jnp.zeros((tq, 2 * MLA_V_DIM), F32)
            for si, vr in zip(s, v_refs):
                p = jnp.exp(si - mx)
                den = den + p.sum(axis=-1, keepdims=True)
                acc = acc + _dot(p.astype(BF16), vr[0, :, pair * 2 * MLA_V_DIM:(pair + 1) * 2 * MLA_V_DIM])
            outs.append(acc * (1.0 / den))
        o_ref[0, :, pair * 2 * MLA_V_DIM:(pair + 1) * 2 * MLA_V_DIM] = jnp.where(
            lane < MLA_V_DIM, outs[0], outs[1])


def _attention(q, ks, vs):
    b, nq, hw = q.shape
    tq = min(256, nq)
    vw = MLA_HEADS * MLA_V_DIM
    n_seg = len(ks)
    seg_spec = lambda a: pl.BlockSpec((1,) + a.shape[1:], lambda i, j: (i, 0, 0))
    return pl.pallas_call(
        functools.partial(_attn_kernel, n_seg=n_seg),
        grid=(b, nq // tq),
        in_specs=[pl.BlockSpec((1, tq, hw), lambda i, j: (i, j, 0))]
                 + [seg_spec(a) for a in ks] + [seg_spec(a) for a in vs],
        out_specs=pl.BlockSpec((1, tq, vw), lambda i, j: (i, j, 0)),
        out_shape=jax.ShapeDtypeStruct((b, nq, vw), F32),
        compiler_params=_cp(("parallel", "parallel"), VMEM_LIMIT),
        name="mla_attention",
    )(q, *ks, *vs)


def _store_token_tiles(ref, val):
    t = val.shape[0]
    for j in range(SUBLANES):
        ref[pl.ds(j, t, stride=SUBLANES), :] = val[:, j * LANES:(j + 1) * LANES]


def _load_token_tiles(ref, t):
    return jnp.concatenate([ref[pl.ds(j, t, stride=SUBLANES), :] for j in range(SUBLANES)], axis=1)


def _outproj_kernel(cv_ref, rt_ref, ml_ref, x_ref, g1_ref, wo_ref, nw_ref, sh_ref, sc_ref,
                    rwh_ref, rwl_ref, rb_ref, x1_ref, h2_ref, idx_ref, wt_ref):
    c0, c1 = CONV_CH, CONV_CH + RET_VW
    y = (_dot(cv_ref[0].astype(BF16), wo_ref[:c0, :])
         + _dot(rt_ref[0].astype(BF16), wo_ref[c0:c1, :])
         + _dot(ml_ref[0].astype(BF16), wo_ref[c1:, :]))
    x1 = x_ref[0] + g1_ref[0] * y
    x1_ref[0] = x1
    h2 = _rms_mod(x1, nw_ref[...], sh_ref[0], sc_ref[0])
    _store_token_tiles(h2_ref.at[0], h2)
    h_hi, h_lo = _split_bf16(h2)
    logits = _dot(h_hi, rwh_ref[...]) + _dot(h_lo, rwh_ref[...]) + _dot(h_hi, rwl_ref[...])
    scores = _sigmoid(logits)
    sel = scores + rb_ref[...]
    t = scores.shape[0]
    eio = lax.broadcasted_iota(jnp.int32, (t, N_EXPERTS), 1).astype(F32)
    lane = lax.broadcasted_iota(jnp.int32, (t, LANES), 1)
    idx_out = jnp.zeros((t, LANES), jnp.int32)
    wt_out = jnp.zeros((t, LANES), F32)
    wsum = jnp.zeros((t, 1), F32)
    for k in range(TOP_K):
        mx = jnp.max(sel, axis=-1, keepdims=True)
        ik = jnp.min(jnp.where(sel == mx, eio, float(N_EXPERTS)), axis=-1, keepdims=True)
        hit = eio == ik
        wk = jnp.sum(jnp.where(hit, scores, 0.0), axis=-1, keepdims=True)
        sel = jnp.where(hit, -jnp.inf, sel)
        idx_out = jnp.where(lane == k, ik.astype(jnp.int32), idx_out)
        wt_out = jnp.where(lane == k, wk, wt_out)
        wsum = wsum + wk
    idx_ref[0] = idx_out
    wt_ref[0] = wt_out / wsum * ROUTED_SCALE


def _outproj(conv, ret, mla, x, g1, wo, nw, sh, sc, rw_hi, rw_lo, rb):
    b, n, d = x.shape
    t = min(256, n)
    tok = lambda w: pl.BlockSpec((1, t, w), lambda i, j: (i, j, 0))
    per_b = pl.BlockSpec((1, 1, d), lambda i, j: (i, 0, 0))
    const = lambda shape: pl.BlockSpec(shape, lambda i, j: (0, 0))
    return pl.pallas_call(
        _outproj_kernel,
        grid=(b, n // t),
        in_specs=[tok(CONV_CH), tok(RET_VW), tok(MLA_HEADS * MLA_V_DIM), tok(d), per_b,
                  const((d, d)), const((1, d)), per_b, per_b,
                  const((d, N_EXPERTS)), const((d, N_EXPERTS)), const((1, N_EXPERTS))],
        out_specs=[tok(d), pl.BlockSpec((1, t * SUBLANES, LANES), lambda i, j: (i, j, 0)),
                   tok(LANES), tok(LANES)],
        out_shape=[jax.ShapeDtypeStruct((b, n, d), F32),
                   jax.ShapeDtypeStruct((b, n * SUBLANES, LANES), F32),
                   jax.ShapeDtypeStruct((b, n, LANES), jnp.int32),
                   jax.ShapeDtypeStruct((b, n, LANES), F32)],
        compiler_params=_cp(("parallel", "parallel"), VMEM_LIMIT),
        name="outproj_norm2_router",
    )(conv, ret, mla, x, g1, wo, nw.reshape(1, d), sh, sc, rw_hi, rw_lo, rb.reshape(1, N_EXPERTS))


_MOE_ROWS = 16


def _moe_kernel(be_ref, src_ref, cnt_ref, tok_ref, wt_ref, h_ref, wg_ref, wu_ref, wd_ref, o_ref,
                xt_ref, yt_ref, *, blocks_per_tile, dump_row):
    b = pl.program_id(0)

    @pl.when(b == 0)
    def _():
        xt_ref[...] = jnp.zeros(xt_ref.shape, F32)

    @pl.when(b % blocks_per_tile == 0)
    def _():
        o_ref[...] = jnp.zeros(o_ref.shape, F32)

    cnt = cnt_ref[b]

    @pl.when(cnt > 0)
    def _():
        base = src_ref[b]
        last = cnt - 1
        for g0 in range(0, MOE_BLOCK, _MOE_ROWS):
            @pl.when(g0 < cnt)
            def _():
                for mi in range(g0, g0 + _MOE_ROWS):
                    t8 = pl.multiple_of(tok_ref[base + jnp.minimum(mi, last)], SUBLANES)
                    xt_ref[mi * SUBLANES:(mi + 1) * SUBLANES, :] = h_ref[0, pl.ds(t8, SUBLANES), :]
        x = jnp.concatenate([xt_ref[pl.ds(j, MOE_BLOCK, stride=SUBLANES), :] for j in range(SUBLANES)],
                            axis=1).astype(BF16)
        g = _dot(x, wg_ref[0])
        u = _dot(x, wu_ref[0])
        y = _dot((_silu(g) * u).astype(BF16), wd_ref[0])
        for j in range(SUBLANES):
            yt_ref[pl.ds(j * MOE_STRIDE, MOE_BLOCK), :] = y[:, j * LANES:(j + 1) * LANES]
        for g0 in range(0, MOE_BLOCK, _MOE_ROWS):
            @pl.when(g0 < cnt)
            def _():
                rows = []
                for mi in range(g0, g0 + _MOE_ROWS):
                    i = base + jnp.minimum(mi, last)
                    t8 = pl.multiple_of(jnp.where(mi < cnt, tok_ref[i], dump_row), SUBLANES)
                    upd = yt_ref[pl.ds(mi, SUBLANES, stride=MOE_STRIDE), :] * wt_ref[i]
                    rows.append((t8, o_ref[0, pl.ds(t8, SUBLANES), :] + upd))
                for t8, val in rows:
                    o_ref[0, pl.ds(t8, SUBLANES), :] = val


def _moe_tile_size(n_tok):
    for s in (2048, 1024, 512, 256):
        if n_tok % s == 0:
            return s
    raise ValueError(f"token count {n_tok} must be a multiple of 256")


_MOE_SMEM_WORDS = 64 * 1024


def _routed_experts(h3, idx, wts, wg, wu, wd, ts):
    n_tiles = h3.shape[0]
    d = SUBLANES * LANES
    n_assign = ts * TOP_K
    bpt = n_assign // MOE_BLOCK + N_EXPERTS

    e_t = idx.reshape(n_tiles, n_assign).astype(jnp.int32)
    tok8 = jnp.broadcast_to((jnp.arange(n_assign, dtype=jnp.int32) // TOP_K) * SUBLANES, e_t.shape)
    _, s_tok, s_w = lax.sort((e_t, tok8, wts.reshape(n_tiles, n_assign)), dimension=1, num_keys=1,
                             is_stable=True)
    counts = jnp.sum((e_t[:, :, None] == jnp.arange(N_EXPERTS, dtype=jnp.int32)).astype(jnp.int32), axis=1)
    start = jnp.cumsum(counts, axis=1) - counts
    nblk = (counts + MOE_BLOCK - 1) // MOE_BLOCK
    blk_end = jnp.cumsum(nblk, axis=1)
    blk_start = blk_end - nblk
    used = blk_end[:, -1:]
    bi = jnp.broadcast_to(jnp.arange(bpt, dtype=jnp.int32), (n_tiles, bpt))
    bi_c = jnp.minimum(bi, used - 1)
    e_b = jnp.sum((blk_end[:, None, :] <= bi_c[:, :, None]).astype(jnp.int32), axis=2)
    take = lambda a: jnp.take_along_axis(a, e_b, axis=1)
    j = bi_c - take(blk_start)
    src = take(start) + j * MOE_BLOCK
    cnt = jnp.where(bi < used, jnp.clip(take(counts) - j * MOE_BLOCK, 0, MOE_BLOCK), 0)

    n_calls = -(-n_tiles // (_MOE_SMEM_WORDS // n_assign))
    bounds = [n_tiles * c // n_calls for c in range(n_calls + 1)]
    outs = []
    for c in range(n_calls):
        t0, t1 = bounds[c], bounds[c + 1]
        group = t1 - t0
        nb = group * bpt
        src_abs = src[t0:t1] + jnp.arange(group, dtype=jnp.int32)[:, None] * n_assign
        weight = lambda shape: pl.BlockSpec(shape, lambda i, be, sr, cn, tk, wt: (be[i], 0, 0))
        grid_spec = pltpu.PrefetchScalarGridSpec(
            num_scalar_prefetch=5,
            grid=(nb,),
            in_specs=[pl.BlockSpec((1, ts * SUBLANES, LANES),
                                   lambda i, be, sr, cn, tk, wt, t0=t0: (t0 + i // bpt, 0, 0)),
                      weight((1, d, EXPERT_DIM)), weight((1, d, EXPERT_DIM)), weight((1, EXPERT_DIM, d))],
            out_specs=pl.BlockSpec((1, (ts + 1) * SUBLANES, LANES),
                                   lambda i, be, sr, cn, tk, wt: (i // bpt, 0, 0)),
            scratch_shapes=[pltpu.VMEM((MOE_BLOCK * SUBLANES, LANES), F32),
                            pltpu.VMEM((SUBLANES * MOE_STRIDE, LANES), F32)],
        )
        outs.append(pl.pallas_call(
            functools.partial(_moe_kernel, blocks_per_tile=bpt, dump_row=ts * SUBLANES),
            grid_spec=grid_spec,
            out_shape=jax.ShapeDtypeStruct((group, (ts + 1) * SUBLANES, LANES), F32),
            compiler_params=_cp(("arbitrary",), VMEM_LIMIT),
            name="routed_experts",
        )(e_b[t0:t1].reshape(-1), src_abs.reshape(-1), cnt[t0:t1].reshape(-1),
          s_tok[t0:t1].reshape(-1), s_w[t0:t1].reshape(-1), h3, wg, wu, wd))
    return jnp.concatenate(outs, axis=0)


def _ffn_out_kernel(x_ref, h_ref, r_ref, g2_ref, sg_ref, su_ref, sd_ref, fw_ref, o_ref, *, final):
    t = x_ref.shape[0]
    h = _load_token_tiles(h_ref.at[0], t).astype(BF16)
    a = _silu(_dot(h, sg_ref[...])) * _dot(h, su_ref[...])
    y = _load_token_tiles(r_ref.at[0], t) + _dot(a.astype(BF16), sd_ref[...])
    x2 = x_ref[...] + g2_ref[0] * y
    if final:
        var = jnp.mean(x2 * x2, axis=-1, keepdims=True)
        x2 = (x2 * lax.rsqrt(var + NORM_EPS)) * fw_ref[...]
    o_ref[...] = x2


def _ffn_out(x1, h3, routed, ts, g2_rows, tiles_per_row, sg, su, sd, fw, final):
    n_tok, d = x1.shape
    t = 256
    per = ts // t
    last = g2_rows.shape[0] - 1
    tok = pl.BlockSpec((t, d), lambda i: (i, 0))
    tiles = pl.BlockSpec((1, t * SUBLANES, LANES), lambda i: (i // per, i % per, 0))
    const = lambda shape: pl.BlockSpec(shape, lambda i: (0, 0))
    return pl.pallas_call(
        functools.partial(_ffn_out_kernel, final=final),
        grid=(n_tok // t,),
        in_specs=[tok, tiles, tiles,
                  pl.BlockSpec((1, 1, d), lambda i: (jnp.minimum(i // tiles_per_row, last), 0, 0)),
                  const((d, EXPERT_DIM)), const((d, EXPERT_DIM)), const((EXPERT_DIM, d)), const((1, d))],
        out_specs=tok,
        out_shape=jax.ShapeDtypeStruct((n_tok, d), F32),
        compiler_params=_cp(("parallel",), VMEM_LIMIT),
        name="shared_expert_residual",
    )(x1, h3, routed, g2_rows, sg, su, sd, fw.reshape(1, d))


def _rope_tables(rows, dim, group, lo):
    pos_r = jnp.repeat(jnp.arange(rows, dtype=F32), GRID_W)
    pos_c = jnp.tile(jnp.arange(GRID_W, dtype=F32), rows)
    n_freq = dim // 4
    inv = ROPE_BASE ** (-jnp.arange(n_freq, dtype=F32) / n_freq)
    ang = jnp.concatenate([pos_r[:, None] * inv, pos_c[:, None] * inv], axis=-1)
    cos, sin = jnp.cos(ang), jnp.sin(ang)
    n = rows * GRID_W
    half = dim // 2
    cos_g = jnp.ones((n, group), F32).at[:, lo:lo + dim].set(jnp.concatenate([cos, cos], axis=-1))
    sin_g = jnp.zeros((n, group), F32).at[:, lo:lo + dim].set(jnp.concatenate([-sin, sin], axis=-1))
    reps = LANES // group
    return jnp.tile(cos_g, (1, reps)), jnp.tile(sin_g, (1, reps))


def _pad_in_proj(w_in):
    d = w_in.shape[0]
    body = w_in[:, :IN_COLS_PAD - HEAD_PAD]
    kr = w_in[:, IN_COLS_PAD - HEAD_PAD:]
    kr_pad = jnp.zeros((d, HEAD_PAD), w_in.dtype).at[:, MLA_NOPE_DIM:MLA_NOPE_DIM + MLA_ROPE_DIM].set(kr)
    return jnp.concatenate([body, kr_pad], axis=1).astype(BF16)


def _pad_heads(w, width):
    k = w.shape[0]
    w3 = w.reshape(k, MLA_HEADS, width)
    return jnp.zeros((k, MLA_HEADS, HEAD_PAD), w.dtype).at[:, :, :width].set(w3).reshape(
        k, MLA_HEADS * HEAD_PAD).astype(BF16)


def kernel(x, c, ctx, c_ctx, mod_w, mod_b, norm1_w, w_in, conv_w, conv_b, conv_ln_w, conv_ln_b,
           ret_decay_logit, ret_gn_w, q_norm_w, w_uq, kv_norm_w, w_ukv, w_out, norm2_w,
           router_w, router_b, exp_w_gate, exp_w_up, exp_w_down, sh_w_gate, sh_w_up, sh_w_down,
           final_norm_w):
    b, n_lat, d = x.shape
    n_ctx = ctx.shape[1]
    depth = mod_w.shape[0]
    rows = n_lat // GRID_W
    cos_ret, sin_ret = _rope_tables(rows, RET_QK_DIM, RET_QK_DIM, 0)
    cos_mla, sin_mla = _rope_tables(rows, MLA_ROPE_DIM, HEAD_PAD, MLA_NOPE_DIM)

    mod_rows = -(-(b + 1) // SUBLANES) * SUBLANES
    cc = jnp.zeros((mod_rows, d), F32).at[:b].set(c).at[b].set(c_ctx)

    xl, xc = x, ctx
    for i in range(depth):
        last = i == depth - 1
        mod = _modulation(cc, mod_w[i], mod_b[i])
        ml = mod[:b].reshape(b, 1, 6, d)
        sh1, sc1, g1, sh2, sc2, g2 = [ml[:, :, j, :] for j in range(6)]
        mc = jnp.broadcast_to(mod[b].reshape(1, 1, 6, d), (b, 1, 6, d))
        csh1, csc1, cg1, csh2, csc2, cg2 = [mc[:, :, j, :] for j in range(6)]

        w_in_p = _pad_in_proj(w_in[i])
        ul, rl, mlat = _inproj(xl, norm1_w[i], sh1, sc1, w_in_p)
        uc, rc, mctx = _inproj(xc, norm1_w[i], csh1, csc1, w_in_p)

        conv_l = _conv(ul, conv_w[i], conv_b[i], conv_ln_w[i], conv_ln_b[i])
        log_gamma = jax.nn.log_sigmoid(ret_decay_logit[i].astype(F32))
        ret_l, ret_c = _retention(rl, rc, log_gamma, cos_ret, sin_ret, ret_gn_w[i], not last)

        wq_p = _pad_heads(w_uq[i], MLA_NOPE_DIM + MLA_ROPE_DIM)
        wkv = w_ukv[i].reshape(MLA_KV_RANK, MLA_HEADS, MLA_NOPE_DIM + MLA_V_DIM)
        wk_p = _pad_heads(wkv[:, :, :MLA_NOPE_DIM].reshape(MLA_KV_RANK, -1), MLA_NOPE_DIM)
        wv = wkv[:, :, MLA_NOPE_DIM:].reshape(MLA_KV_RANK, -1).astype(BF16)
        ql, kl, vl = _mla_proj(mlat, q_norm_w[i], kv_norm_w[i], wq_p, wk_p, wv, cos_mla, sin_mla, True)
        qc, kc, vc = _mla_proj(mctx, q_norm_w[i], kv_norm_w[i], wq_p, wk_p, wv,
                               cos_mla[:n_ctx], sin_mla[:n_ctx], False)
        mla_l = _attention(ql, [kc, kl], [vc, vl])

        wo = w_out[i].astype(BF16)
        rw_hi = router_w[i].astype(BF16)
        rw_lo = (router_w[i] - rw_hi.astype(F32)).astype(BF16)
        x1l, h2l, idx_l, wt_l = _outproj(conv_l, ret_l, mla_l, xl, g1, wo, norm2_w[i], sh2, sc2,
                                         rw_hi, rw_lo, router_b[i])
        n_l = b * n_lat
        if last:
            n_tok = n_l
            x1 = x1l.reshape(n_l, d)
            h2t = h2l.reshape(n_l * SUBLANES, LANES)
            idx = idx_l.reshape(n_l, LANES)[:, :TOP_K]
            wts = wt_l.reshape(n_l, LANES)[:, :TOP_K]
            g2_rows = g2
        else:
            conv_c = _conv(uc, conv_w[i], conv_b[i], conv_ln_w[i], conv_ln_b[i])
            mla_c = _attention(qc, [kc], [vc])
            x1c, h2c, idx_c, wt_c = _outproj(conv_c, ret_c, mla_c, xc, cg1, wo, norm2_w[i], csh2, csc2,
                                             rw_hi, rw_lo, router_b[i])
            n_c = b * n_ctx
            n_tok = n_l + n_c
            x1 = jnp.concatenate([x1l.reshape(n_l, d), x1c.reshape(n_c, d)], axis=0)
            h2t = jnp.concatenate([h2l.reshape(n_l * SUBLANES, LANES),
                                   h2c.reshape(n_c * SUBLANES, LANES)], axis=0)
            idx = jnp.concatenate([idx_l.reshape(n_l, LANES), idx_c.reshape(n_c, LANES)], axis=0)[:, :TOP_K]
            wts = jnp.concatenate([wt_l.reshape(n_l, LANES), wt_c.reshape(n_c, LANES)], axis=0)[:, :TOP_K]
            g2_rows = jnp.concatenate([g2, cg2[:1]], axis=0)

        ts = _moe_tile_size(n_tok)
        h3 = h2t.reshape(n_tok // ts, ts * SUBLANES, LANES)
        routed = _routed_experts(h3, idx, wts, exp_w_gate[i].astype(BF16), exp_w_up[i].astype(BF16),
                                 exp_w_down[i].astype(BF16), ts)
        x2 = _ffn_out(x1, h3, routed, ts, g2_rows, n_lat // 256, sh_w_gate[i].astype(BF16),
                      sh_w_up[i].astype(BF16), sh_w_down[i].astype(BF16), final_norm_w, last)
        xl = x2[:n_l].reshape(b, n_lat, d)
        if not last:
            xc = x2[n_l:].reshape(b, n_ctx, d)
    return xl
```

```python
import functools

import jax
import jax.numpy as jnp
from jax import lax
from jax.experimental import pallas as pl
from jax.experimental.pallas import tpu as pltpu

F32 = jnp.float32
BF16 = jnp.bfloat16

D_MODEL = 1024
GRID_W = 64
CONV_CH = 256
CONV_K = 31
RET_HEADS = 4
RET_QK_DIM = 32
RET_V_DIM = 64
RET_CHUNK = 128
MLA_HEADS = 8
MLA_NOPE_DIM = 64
MLA_ROPE_DIM = 32
MLA_V_DIM = 64
MLA_Q_RANK = 256
MLA_KV_RANK = 128
ROPE_BASE = 10000.0
N_EXPERTS = 64
TOP_K = 6
EXPERT_DIM = 256
ROUTED_SCALE = 2.5
NORM_EPS = 1e-6

LANES = 128
SUBLANES = 8
HEAD_PAD = 128
RET_W = RET_HEADS * RET_QK_DIM
RET_VW = RET_HEADS * RET_V_DIM
IN_COLS_PAD = 2 * CONV_CH + 2 * RET_W + 2 * RET_VW + MLA_Q_RANK + MLA_KV_RANK + HEAD_PAD
MOE_BLOCK = 256
MOE_STRIDE = MOE_BLOCK + SUBLANES
VMEM_LIMIT = 56 * 1024 * 1024


def _cp(sem, vmem=None):
    return pltpu.CompilerParams(dimension_semantics=sem, vmem_limit_bytes=vmem)


def _dot(a, b):
    return jnp.dot(a, b, preferred_element_type=F32)


def _split_bf16(a):
    hi = a.astype(BF16)
    lo = (a - hi.astype(F32)).astype(BF16)
    return hi, lo


def _sigmoid(x):
    return 1.0 / (1.0 + jnp.exp(-x))


def _silu(x):
    return x * _sigmoid(x)


def _mod_kernel(c_ref, w_ref, b_ref, o_ref):
    a_hi, a_lo = _split_bf16(_silu(c_ref[...]))
    w_hi, w_lo = _split_bf16(w_ref[...])
    o_ref[...] = _dot(a_hi, w_hi) + _dot(a_lo, w_hi) + _dot(a_hi, w_lo) + b_ref[...]


def _modulation(cc, w, b):
    rows, d = cc.shape
    n = w.shape[1]
    bn = 1536
    return pl.pallas_call(
        _mod_kernel,
        grid=(n // bn,),
        in_specs=[pl.BlockSpec((rows, d), lambda j: (0, 0)),
                  pl.BlockSpec((d, bn), lambda j: (0, j)),
                  pl.BlockSpec((1, bn), lambda j: (0, j))],
        out_specs=pl.BlockSpec((rows, bn), lambda j: (0, j)),
        out_shape=jax.ShapeDtypeStruct((rows, n), F32),
        compiler_params=_cp(("arbitrary",), VMEM_LIMIT),
        name="modulation",
    )(cc, w, b.reshape(1, n))


def _rms_mod(x, nw, sh, sc):
    var = jnp.mean(x * x, axis=-1, keepdims=True)
    h = (x * lax.rsqrt(var + NORM_EPS)) * nw
    return h * (1.0 + sc) + sh


def _inproj_kernel(x_ref, nw_ref, sh_ref, sc_ref, w_ref, u_ref, r_ref, m_ref):
    h = _rms_mod(x_ref[...], nw_ref[...], sh_ref[0], sc_ref[0])
    z = _dot(h.astype(BF16), w_ref[...])
    c0 = 2 * CONV_CH
    c1 = c0 + 2 * RET_W + 2 * RET_VW
    u_ref[0] = z[:, :c0]
    r_ref[0] = z[:, c0:c1]
    m_ref[0] = z[:, c1:]


def _inproj(x_flat, row0, b, n, nw, sh, sc, w_pad):
    d = x_flat.shape[1]
    t = min(512, n)
    nt = n // t
    off = row0 // t
    wu, wr, wm = 2 * CONV_CH, 2 * RET_W + 2 * RET_VW, MLA_Q_RANK + MLA_KV_RANK + HEAD_PAD
    tok = lambda w: pl.BlockSpec((1, t, w), lambda i, j: (i, j, 0))
    per_b = pl.BlockSpec((1, 1, d), lambda i, j: (i, 0, 0))
    return pl.pallas_call(
        _inproj_kernel,
        grid=(b, nt),
        in_specs=[pl.BlockSpec((t, d), lambda i, j: (off + i * nt + j, 0)),
                  pl.BlockSpec((1, d), lambda i, j: (0, 0)), per_b, per_b,
                  pl.BlockSpec((d, IN_COLS_PAD), lambda i, j: (0, 0))],
        out_specs=[tok(wu), tok(wr), tok(wm)],
        out_shape=[jax.ShapeDtypeStruct((b, n, wu), F32),
                   jax.ShapeDtypeStruct((b, n, wr), F32),
                   jax.ShapeDtypeStruct((b, n, wm), F32)],
        compiler_params=_cp(("parallel", "parallel"), VMEM_LIMIT),
        name="norm1_inproj",
    )(x_flat, nw.reshape(1, d), sh, sc, w_pad)


_CONV_PAD = 16
_CONV_ROWS = 128


def _conv_kernel(u_ref, cw_ref, cb_ref, lw_ref, lb_ref, o_ref, hp_ref, *, n):
    c = CONV_CH
    hp_ref[0:_CONV_PAD, :] = jnp.zeros((_CONV_PAD, c), F32)
    hp_ref[n + _CONV_PAD:n + 2 * _CONV_PAD, :] = jnp.zeros((_CONV_PAD, c), F32)

    def glu(i, carry):
        r = pl.multiple_of(i * _CONV_ROWS, _CONV_ROWS)
        u = u_ref[0, pl.ds(r, _CONV_ROWS), :]
        hp_ref[pl.ds(r + _CONV_PAD, _CONV_ROWS), :] = u[:, :c] * _sigmoid(u[:, c:])
        return carry

    lax.fori_loop(0, n // _CONV_ROWS, glu, 0)

    def conv(i, carry):
        r = pl.multiple_of(i * _CONV_ROWS, _CONV_ROWS)
        acc = jnp.zeros((_CONV_ROWS, c), F32)
        base = _CONV_PAD - CONV_K // 2
        for q in range((base + CONV_K - 1) // SUBLANES + 1):
            win = hp_ref[pl.ds(r + q * SUBLANES, _CONV_ROWS + SUBLANES), :]
            for s in range(SUBLANES):
                k = q * SUBLANES + s - base
                if 0 <= k < CONV_K:
                    acc = acc + cw_ref[k:k + 1, :] * win[s:s + _CONV_ROWS, :]
        hcv = acc + cb_ref[...]
        mu = jnp.mean(hcv, axis=-1, keepdims=True)
        dlt = hcv - mu
        var = jnp.mean(dlt * dlt, axis=-1, keepdims=True)
        y = (dlt * lax.rsqrt(var + NORM_EPS)) * lw_ref[...] + lb_ref[...]
        o_ref[0, pl.ds(r, _CONV_ROWS), :] = _silu(y)
        return carry

    lax.fori_loop(0, n // _CONV_ROWS, conv, 0)


def _conv(u, cw, cb, lw, lb):
    b, n, _ = u.shape
    c = CONV_CH
    vec = pl.BlockSpec((1, c), lambda i: (0, 0))
    return pl.pallas_call(
        functools.partial(_conv_kernel, n=n),
        grid=(b,),
        in_specs=[pl.BlockSpec((1, n, 2 * c), lambda i: (i, 0, 0)),
                  pl.BlockSpec((CONV_K, c), lambda i: (0, 0)), vec, vec, vec],
        out_specs=pl.BlockSpec((1, n, c), lambda i: (i, 0, 0)),
        out_shape=jax.ShapeDtypeStruct((b, n, c), F32),
        scratch_shapes=[pltpu.VMEM((n + 2 * _CONV_PAD, c), F32)],
        compiler_params=_cp(("parallel",), VMEM_LIMIT),
        name="conformer_conv",
    )(u, cw, cb.reshape(1, c), lw.reshape(1, c), lb.reshape(1, c))


def _rope_partner(x, group, lo):
    half = 16
    lane = lax.broadcasted_iota(jnp.int32, x.shape, 1) % group
    first = (lane >= lo) & (lane < lo + half)
    return jnp.where(first, pltpu.roll(x, LANES - half, 1), pltpu.roll(x, half, 1))


def _ret_kernel(lg_ref, rl_ref, rc_ref, cos_ref, sin_ref, gn_ref, avg_ref, ol_ref, oc_ref,
                q_s, k_s, yl_s, yc_s, dst_s, dq_s, dk_s, dch_s, *, n_lat, n_ctx, ctx_out):
    ch = RET_CHUNK
    lane_q = lax.broadcasted_iota(jnp.int32, (1, RET_W), 1) // RET_QK_DIM
    lane_v = lax.broadcasted_iota(jnp.int32, (1, RET_VW), 1) // RET_V_DIM
    row_h = lax.broadcasted_iota(jnp.int32, (RET_W, 1), 0) // RET_QK_DIM
    bd = (row_h == lane_v).astype(F32)
    qmask = [(lane_q == h).astype(F32) for h in range(RET_HEADS)]
    vmask = [(lane_v == h).astype(F32) for h in range(RET_HEADS)]

    ri = lax.broadcasted_iota(jnp.int32, (ch, ch), 0).astype(F32)
    ci = lax.broadcasted_iota(jnp.int32, (ch, ch), 1).astype(F32)
    rowi = lax.broadcasted_iota(jnp.int32, (ch, 1), 0).astype(F32)
    for d in range(2):
        lg_lane = jnp.zeros((1, RET_W), F32)
        lg_row = jnp.zeros((RET_W, 1), F32)
        for h in range(RET_HEADS):
            lg = lg_ref[d * RET_HEADS + h]
            lg_lane = jnp.where(lane_q == h, lg, lg_lane)
            lg_row = jnp.where(row_h == h, lg, lg_row)
            rel = (ri - ci) if d == 0 else (ci - ri)
            dst_s[d, h * ch:(h + 1) * ch, :] = jnp.where(
                rel >= 0, jnp.exp(lg * jnp.maximum(rel, 0.0)), 0.0)
        if d == 0:
            dq_s[d] = jnp.exp(lg_lane * (rowi + 1.0))
            dk_s[d] = jnp.exp(lg_lane * (ch - 1.0 - rowi))
        else:
            dq_s[d] = jnp.exp(lg_lane * (ch - rowi))
            dk_s[d] = jnp.exp(lg_lane * rowi)
        dch_s[d] = jnp.exp(lg_row * float(ch)) * jnp.ones((1, RET_VW), F32)

    kscale = RET_QK_DIM ** -0.5

    def stage(src_ref, n, rope):
        def body(i, carry):
            r = pl.multiple_of(i * ch, ch)
            q = src_ref[0, pl.ds(r, ch), 0:RET_W]
            k = src_ref[0, pl.ds(r, ch), RET_W:2 * RET_W] * kscale
            if rope:
                cs = cos_ref[pl.ds(r, ch), :]
                sn = sin_ref[pl.ds(r, ch), :]
                q = q * cs + _rope_partner(q, RET_QK_DIM, 0) * sn
                k = k * cs + _rope_partner(k, RET_QK_DIM, 0) * sn
            q_s[pl.ds(r, ch), :] = q
            k_s[pl.ds(r, ch), :] = k
            return carry
        lax.fori_loop(0, n // ch, body, 0)

    def scan(src_ref, y_ref, n, d, s0, accumulate):
        nchunks = n // ch

        def body(i, s):
            c = i if d == 0 else nchunks - 1 - i
            r = pl.multiple_of(c * ch, ch)
            qc = q_s[pl.ds(r, ch), :]
            kc = k_s[pl.ds(r, ch), :]
            vc = src_ref[0, pl.ds(r, ch), 2 * RET_W:2 * RET_W + RET_VW].astype(BF16)
            qst = jnp.concatenate([qc * qmask[h] for h in range(RET_HEADS)], axis=0).astype(BF16)
            inner = lax.dot_general(qst, kc.astype(BF16), (((1,), (1,)), ((), ())),
                                    preferred_element_type=F32) * dst_s[d]
            o = _dot(inner.astype(BF16), vc)
            y = _dot((qc * dq_s[d]).astype(BF16), s.astype(BF16))
            for h in range(RET_HEADS):
                y = y + o[h * ch:(h + 1) * ch, :] * vmask[h]
            kd_t = (kc * dk_s[d]).T.astype(BF16)
            s_new = s * dch_s[d] + _dot(kd_t, vc) * bd
            if accumulate:
                y_ref[pl.ds(r, ch), :] = y_ref[pl.ds(r, ch), :] + y
            else:
                y_ref[pl.ds(r, ch), :] = y
            return s_new

        return lax.fori_loop(0, nchunks, body, s0)

    def finish(src_ref, y_ref, out_ref, n):
        def body(i, carry):
            r = pl.multiple_of(i * ch, ch)
            y = y_ref[pl.ds(r, ch), :]
            y_hi, y_lo = _split_bf16(y)
            mu = _dot(y_hi, avg_ref[...]) + _dot(y_lo, avg_ref[...])
            dlt = y - mu
            d_hi, d_lo = _split_bf16(dlt * dlt)
            var = _dot(d_hi, avg_ref[...]) + _dot(d_lo, avg_ref[...])
            yn = dlt * lax.rsqrt(var + NORM_EPS)
            g = src_ref[0, pl.ds(r, ch), 2 * RET_W + RET_VW:2 * RET_W + 2 * RET_VW]
            out_ref[0, pl.ds(r, ch), :] = _silu(g) * (yn * gn_ref[...])
            return carry
        lax.fori_loop(0, n // ch, body, 0)

    s_zero = jnp.zeros((RET_W, RET_VW), F32)
    stage(rc_ref, n_ctx, False)
    sc_f = scan(rc_ref, yc_s, n_ctx, 0, s_zero, False)
    sc_b = scan(rc_ref, yc_s, n_ctx, 1, s_zero, True)
    if ctx_out:
        finish(rc_ref, yc_s, oc_ref, n_ctx)
    else:
        oc_ref[...] = jnp.zeros(oc_ref.shape, F32)
    stage(rl_ref, n_lat, True)
    scan(rl_ref, yl_s, n_lat, 0, sc_f, False)
    scan(rl_ref, yl_s, n_lat, 1, sc_b, True)
    finish(rl_ref, yl_s, ol_ref, n_lat)


def _retention(r_lat, r_ctx, log_gamma, cos_t, sin_t, gn_w, ctx_out):
    b, n_lat, w = r_lat.shape
    n_ctx = r_ctx.shape[1]
    grp = jnp.arange(RET_VW) // RET_V_DIM
    avg = ((grp[:, None] == grp[None, :]).astype(F32) / RET_V_DIM).astype(BF16)
    tab = pl.BlockSpec((n_lat, RET_W), lambda i, lg: (0, 0))
    grid_spec = pltpu.PrefetchScalarGridSpec(
        num_scalar_prefetch=1,
        grid=(b,),
        in_specs=[pl.BlockSpec((1, n_lat, w), lambda i, lg: (i, 0, 0)),
                  pl.BlockSpec((1, n_ctx, w), lambda i, lg: (i, 0, 0)),
                  tab, tab,
                  pl.BlockSpec((1, RET_VW), lambda i, lg: (0, 0)),
                  pl.BlockSpec((RET_VW, RET_VW), lambda i, lg: (0, 0))],
        out_specs=[pl.BlockSpec((1, n_lat, RET_VW), lambda i, lg: (i, 0, 0)),
                   pl.BlockSpec((1, n_ctx, RET_VW), lambda i, lg: (i, 0, 0))],
        scratch_shapes=[pltpu.VMEM((max(n_lat, n_ctx), RET_W), F32),
                        pltpu.VMEM((max(n_lat, n_ctx), RET_W), F32),
                        pltpu.VMEM((n_lat, RET_VW), F32),
                        pltpu.VMEM((n_ctx, RET_VW), F32),
                        pltpu.VMEM((2, RET_HEADS * RET_CHUNK, RET_CHUNK), F32),
                        pltpu.VMEM((2, RET_CHUNK, RET_W), F32),
                        pltpu.VMEM((2, RET_CHUNK, RET_W), F32),
                        pltpu.VMEM((2, RET_W, RET_VW), F32)],
    )
    return pl.pallas_call(
        functools.partial(_ret_kernel, n_lat=n_lat, n_ctx=n_ctx, ctx_out=ctx_out),
        grid_spec=grid_spec,
        out_shape=[jax.ShapeDtypeStruct((b, n_lat, RET_VW), F32),
                   jax.ShapeDtypeStruct((b, n_ctx, RET_VW), F32)],
        compiler_params=_cp(("parallel",), VMEM_LIMIT),
        name="retention",
    )(log_gamma.reshape(-1), r_lat, r_ctx, cos_t, sin_t, gn_w.reshape(1, RET_VW), avg)


def _mla_proj_kernel(m_ref, qn_ref, kvn_ref, wq_ref, wk_ref, wv_ref, cos_ref, sin_ref,
                     q_ref, k_ref, v_ref, *, rope):
    m = m_ref[0]
    cq = m[:, :MLA_Q_RANK]
    ckv = m[:, MLA_Q_RANK:MLA_Q_RANK + MLA_KV_RANK]
    kr = m[:, MLA_Q_RANK + MLA_KV_RANK:]

    def rms(x, w):
        var = jnp.mean(x * x, axis=-1, keepdims=True)
        return (x * lax.rsqrt(var + NORM_EPS)) * w

    q = _dot(rms(cq, qn_ref[...]).astype(BF16), wq_ref[...])
    ckv_n = rms(ckv, kvn_ref[...]).astype(BF16)
    k = _dot(ckv_n, wk_ref[...])
    v_ref[0] = _dot(ckv_n, wv_ref[...]).astype(BF16)
    if rope:
        cs = cos_ref[...]
        sn = sin_ref[...]
        kr = kr * cs + _rope_partner(kr, HEAD_PAD, MLA_NOPE_DIM) * sn
    scale = (MLA_NOPE_DIM + MLA_ROPE_DIM) ** -0.5
    for h in range(MLA_HEADS):
        sl = slice(h * HEAD_PAD, (h + 1) * HEAD_PAD)
        qh = q[:, sl]
        if rope:
            qh = qh * cs + _rope_partner(qh, HEAD_PAD, MLA_NOPE_DIM) * sn
        q_ref[0, :, sl] = (qh * scale).astype(BF16)
        k_ref[0, :, sl] = (k[:, sl] + kr).astype(BF16)


def _mla_proj(m, qn_w, kvn_w, wq_pad, wk_pad, wv, cos_t, sin_t, rope):
    b, n, w = m.shape
    t = min(512, n)
    hw = MLA_HEADS * HEAD_PAD
    vw = MLA_HEADS * MLA_V_DIM
    const = lambda shape: pl.BlockSpec(shape, lambda i, j: (0, 0))
    tab = pl.BlockSpec((t, HEAD_PAD), lambda i, j: (j, 0))
    return pl.pallas_call(
        functools.partial(_mla_proj_kernel, rope=rope),
        grid=(b, n // t),
        in_specs=[pl.BlockSpec((1, t, w), lambda i, j: (i, j, 0)),
                  const((1, MLA_Q_RANK)), const((1, MLA_KV_RANK)),
                  const((MLA_Q_RANK, hw)), const((MLA_KV_RANK, hw)), const((MLA_KV_RANK, vw)),
                  tab, tab],
        out_specs=[pl.BlockSpec((1, t, hw), lambda i, j: (i, j, 0)),
                   pl.BlockSpec((1, t, hw), lambda i, j: (i, j, 0)),
                   pl.BlockSpec((1, t, vw), lambda i, j: (i, j, 0))],
        out_shape=[jax.ShapeDtypeStruct((b, n, hw), BF16),
                   jax.ShapeDtypeStruct((b, n, hw), BF16),
                   jax.ShapeDtypeStruct((b, n, vw), BF16)],
        compiler_params=_cp(("parallel", "parallel"), VMEM_LIMIT),
        name="mla_proj",
    )(m, qn_w.reshape(1, -1), kvn_w.reshape(1, -1), wq_pad, wk_pad, wv, cos_t, sin_t)


def _attn_kernel(*refs, n_seg):
    q_ref = refs[0]
    k_refs = refs[1:1 + n_seg]
    v_refs = refs[1 + n_seg:1 + 2 * n_seg]
    o_ref = refs[1 + 2 * n_seg]
    tq = q_ref.shape[1]
    lane = lax.broadcasted_iota(jnp.int32, (tq, 2 * MLA_V_DIM), 1)
    for pair in range(MLA_HEADS // 2):
        outs = []
        for h in (2 * pair, 2 * pair + 1):
            sl = slice(h * HEAD_PAD, (h + 1) * HEAD_PAD)
            qh = q_ref[0, :, sl]
            s = [lax.dot_general(qh, kr[0, :, sl], (((1,), (1,)), ((), ())),
                                 preferred_element_type=F32) for kr in k_refs]
            mx = s[0].max(axis=-1, keepdims=True)
            for si in s[1:]:
                mx = jnp.maximum(mx, si.max(axis=-1, keepdims=True))
            den = jnp.zeros((tq, 1), F32)
            acc = jnp.zeros((tq, 2 * MLA_V_DIM), F32)
            for si, vr in zip(s, v_refs):
                p = jnp.exp(si - mx)
                den = den + p.sum(axis=-1, keepdims=True)
                acc = acc + _dot(p.astype(BF16), vr[0, :, pair * 2 * MLA_V_DIM:(pair + 1) * 2 * MLA_V_DIM])
            outs.append(acc * (1.0 / den))
        o_ref[0, :, pair * 2 * MLA_V_DIM:(pair + 1) * 2 * MLA_V_DIM] = jnp.where(
            lane < MLA_V_DIM, outs[0], outs[1])


def _attention(q, ks, vs):
    b, nq, hw = q.shape
    tq = min(256, nq)
    vw = MLA_HEADS * MLA_V_DIM
    n_seg = len(ks)
    seg_spec = lambda a: pl.BlockSpec((1,) + a.shape[1:], lambda i, j: (i, 0, 0))
    return pl.pallas_call(
        functools.partial(_attn_kernel, n_seg=n_seg),
        grid=(b, nq // tq),
        in_specs=[pl.BlockSpec((1, tq, hw), lambda i, j: (i, j, 0))]
                 + [seg_spec(a) for a in ks] + [seg_spec(a) for a in vs],
        out_specs=pl.BlockSpec((1, tq, vw), lambda i, j: (i, j, 0)),
        out_shape=jax.ShapeDtypeStruct((b, nq, vw), F32),
        compiler_params=_cp(("parallel", "parallel"), VMEM_LIMIT),
        name="mla_attention",
    )(q, *ks, *vs)


def _store_token_tiles(ref, val):
    t = val.shape[0]
    for j in range(SUBLANES):
        ref[pl.ds(j, t, stride=SUBLANES), :] = val[:, j * LANES:(j + 1) * LANES]


def _load_token_tiles(ref, t):
    return jnp.concatenate([ref[pl.ds(j, t, stride=SUBLANES), :] for j in range(SUBLANES)], axis=1)


def _outproj_kernel(cv_ref, rt_ref, ml_ref, x_ref, g1_ref, wo_ref, nw_ref, sh_ref, sc_ref,
                    rwh_ref, rwl_ref, rb_ref, *rest):
    x1_ref, h2_ref, idx_ref, wt_ref = rest[-4:]
    c0, c1 = CONV_CH, CONV_CH + RET_VW
    y = (_dot(cv_ref[0].astype(BF16), wo_ref[:c0, :])
         + _dot(rt_ref[0].astype(BF16), wo_ref[c0:c1, :])
         + _dot(ml_ref[0].astype(BF16), wo_ref[c1:, :]))
    x1 = x_ref[...] + g1_ref[0] * y
    x1_ref[...] = x1
    h2 = _rms_mod(x1, nw_ref[...], sh_ref[0], sc_ref[0])
    _store_token_tiles(h2_ref, h2)
    h_hi, h_lo = _split_bf16(h2)
    logits = _dot(h_hi, rwh_ref[...]) + _dot(h_lo, rwh_ref[...]) + _dot(h_hi, rwl_ref[...])
    scores = _sigmoid(logits)
    sel = scores + rb_ref[...]
    t = scores.shape[0]
    eio = lax.broadcasted_iota(jnp.int32, (t, N_EXPERTS), 1).astype(F32)
    lane = lax.broadcasted_iota(jnp.int32, (t, LANES), 1)
    idx_out = jnp.zeros((t, LANES), jnp.int32)
    wt_out = jnp.zeros((t, LANES), F32)
    wsum = jnp.zeros((t, 1), F32)
    for k in range(TOP_K):
        mx = jnp.max(sel, axis=-1, keepdims=True)
        ik = jnp.min(jnp.where(sel == mx, eio, float(N_EXPERTS)), axis=-1, keepdims=True)
        hit = eio == ik
        wk = jnp.sum(jnp.where(hit, scores, 0.0), axis=-1, keepdims=True)
        sel = jnp.where(hit, -jnp.inf, sel)
        idx_out = jnp.where(lane == k, ik.astype(jnp.int32), idx_out)
        wt_out = jnp.where(lane == k, wk, wt_out)
        wsum = wsum + wk
    idx_ref[...] = idx_out
    wt_ref[...] = wt_out / wsum * ROUTED_SCALE


def _outproj(conv, ret, mla, x_flat, x_row0, g1, wo, nw, sh, sc, rw_hi, rw_lo, rb, n_total, row0, carry):
    b, n, _ = conv.shape
    d = x_flat.shape[1]
    x = x_flat
    t = min(256, n)
    nt = n // t
    off = row0 // t
    x_off = x_row0 // t
    tok = lambda w: pl.BlockSpec((1, t, w), lambda i, j: (i, j, 0))
    per_b = pl.BlockSpec((1, 1, d), lambda i, j: (i, 0, 0))
    const = lambda shape: pl.BlockSpec(shape, lambda i, j: (0, 0))
    flat = lambda rows, w: pl.BlockSpec((rows, w), lambda i, j: (off + i * nt + j, 0))
    x_spec = pl.BlockSpec((t, d), lambda i, j: (x_off + i * nt + j, 0))
    in_specs = [tok(CONV_CH), tok(RET_VW), tok(MLA_HEADS * MLA_V_DIM), x_spec, per_b,
                const((d, d)), const((1, d)), per_b, per_b,
                const((d, N_EXPERTS)), const((d, N_EXPERTS)), const((1, N_EXPERTS))]
    operands = [conv, ret, mla, x, g1, wo, nw.reshape(1, d), sh, sc, rw_hi, rw_lo, rb.reshape(1, N_EXPERTS)]
    aliases = {}
    if carry is not None:
        aliases = {len(operands) + k: k for k in range(len(carry))}
        in_specs += [pl.BlockSpec(memory_space=pl.ANY)] * len(carry)
        operands += list(carry)
    return pl.pallas_call(
        _outproj_kernel,
        grid=(b, nt),
        in_specs=in_specs,
        out_specs=[flat(t, d), flat(t * SUBLANES, LANES), flat(t, LANES), flat(t, LANES)],
        out_shape=[jax.ShapeDtypeStruct((n_total, d), F32),
                   jax.ShapeDtypeStruct((n_total * SUBLANES, LANES), F32),
                   jax.ShapeDtypeStruct((n_total, LANES), jnp.int32),
                   jax.ShapeDtypeStruct((n_total, LANES), F32)],
        input_output_aliases=aliases,
        compiler_params=_cp(("parallel", "parallel"), VMEM_LIMIT),
        name="outproj_norm2_router",
    )(*operands)


_MOE_ROWS = 16


def _moe_kernel(be_ref, src_ref, cnt_ref, tok_ref, wt_ref, h_ref, wg_ref, wu_ref, wd_ref, *rest,
                blocks_per_tile, dump_row):
    o_ref, xt_ref, yt_ref = rest[-3:]
    b = pl.program_id(0)

    @pl.when(b == 0)
    def _():
        xt_ref[...] = jnp.zeros(xt_ref.shape, F32)

    @pl.when(b % blocks_per_tile == 0)
    def _():
        o_ref[...] = jnp.zeros(o_ref.shape, F32)

    cnt = cnt_ref[b]

    @pl.when(cnt > 0)
    def _():
        base = src_ref[b]
        last = cnt - 1
        for g0 in range(0, MOE_BLOCK, _MOE_ROWS):
            @pl.when(g0 < cnt)
            def _():
                for mi in range(g0, g0 + _MOE_ROWS):
                    t8 = pl.multiple_of(tok_ref[base + jnp.minimum(mi, last)], SUBLANES)
                    xt_ref[mi * SUBLANES:(mi + 1) * SUBLANES, :] = h_ref[0, pl.ds(t8, SUBLANES), :]
        x = jnp.concatenate([xt_ref[pl.ds(j, MOE_BLOCK, stride=SUBLANES), :] for j in range(SUBLANES)],
                            axis=1).astype(BF16)
        g = _dot(x, wg_ref[...])
        u = _dot(x, wu_ref[...])
        y = _dot((_silu(g) * u).astype(BF16), wd_ref[...])
        for j in range(SUBLANES):
            yt_ref[pl.ds(j * MOE_STRIDE, MOE_BLOCK), :] = y[:, j * LANES:(j + 1) * LANES]
        def scatter_rows(g0, partial):
            rows = []
            for mi in range(g0, g0 + _MOE_ROWS):
                if partial:
                    i = base + jnp.minimum(mi, last)
                    t8 = pl.multiple_of(jnp.where(mi < cnt, tok_ref[i], dump_row), SUBLANES)
                else:
                    i = base + mi
                    t8 = pl.multiple_of(tok_ref[i], SUBLANES)
                upd = yt_ref[pl.ds(mi, SUBLANES, stride=MOE_STRIDE), :] * wt_ref[i]
                rows.append((t8, o_ref[0, pl.ds(t8, SUBLANES), :] + upd))
            for t8, val in rows:
                o_ref[0, pl.ds(t8, SUBLANES), :] = val

        for g0 in range(0, MOE_BLOCK, _MOE_ROWS):
            pl.when(g0 + _MOE_ROWS <= cnt)(functools.partial(scatter_rows, g0, False))
            pl.when((g0 < cnt) & (g0 + _MOE_ROWS > cnt))(functools.partial(scatter_rows, g0, True))


def _moe_tile_size(n_tok):
    for s in (4096, 2048, 1024, 512, 256):
        if n_tok % s == 0:
            return s
    raise ValueError(f"token count {n_tok} must be a multiple of 256")


_MOE_SMEM_WORDS = 64 * 1024


def _routed_experts(h3, idx, wts, wg, wu, wd, layer, ts):
    n_tiles = h3.shape[0]
    d = SUBLANES * LANES
    n_assign = ts * TOP_K
    bpt = n_assign // MOE_BLOCK + N_EXPERTS

    e_t = idx.reshape(n_tiles, n_assign).astype(jnp.int32)
    tok8 = jnp.broadcast_to((jnp.arange(n_assign, dtype=jnp.int32) // TOP_K) * SUBLANES, e_t.shape)
    _, s_tok, s_w = lax.sort((e_t, tok8, wts.reshape(n_tiles, n_assign)), dimension=1, num_keys=1,
                             is_stable=True)
    counts = jnp.sum((e_t[:, :, None] == jnp.arange(N_EXPERTS, dtype=jnp.int32)).astype(jnp.int32), axis=1)
    start = jnp.cumsum(counts, axis=1) - counts
    nblk = (counts + MOE_BLOCK - 1) // MOE_BLOCK
    blk_end = jnp.cumsum(nblk, axis=1)
    blk_start = blk_end - nblk
    used = blk_end[:, -1:]
    bi = jnp.broadcast_to(jnp.arange(bpt, dtype=jnp.int32), (n_tiles, bpt))
    bi_c = jnp.minimum(bi, used - 1)
    e_b = jnp.sum((blk_end[:, None, :] <= bi_c[:, :, None]).astype(jnp.int32), axis=2)
    take = lambda a: jnp.take_along_axis(a, e_b, axis=1)
    j = bi_c - take(blk_start)
    src = take(start) + j * MOE_BLOCK
    cnt = jnp.where(bi < used, jnp.clip(take(counts) - j * MOE_BLOCK, 0, MOE_BLOCK), 0)

    n_calls = -(-n_tiles // (_MOE_SMEM_WORDS // n_assign))
    bounds = [n_tiles * c // n_calls for c in range(n_calls + 1)]
    out = None
    n_pref = 5
    for c in range(n_calls):
        t0, t1 = bounds[c], bounds[c + 1]
        group = t1 - t0
        nb = group * bpt
        src_abs = src[t0:t1] + jnp.arange(group, dtype=jnp.int32)[:, None] * n_assign
        weight = lambda shape: pl.BlockSpec((None, None) + shape,
                                            lambda i, be, sr, cn, tk, wt: (layer, be[i], 0, 0))
        tile_map = lambda i, be, sr, cn, tk, wt, t0=t0: (t0 + i // bpt, 0, 0)
        in_specs = [pl.BlockSpec((1, ts * SUBLANES, LANES), tile_map, pipeline_mode=pl.Buffered(1)),
                    weight((d, EXPERT_DIM)), weight((d, EXPERT_DIM)), weight((EXPERT_DIM, d))]
        operands = [e_b[t0:t1].reshape(-1), src_abs.reshape(-1), cnt[t0:t1].reshape(-1),
                    s_tok[t0:t1].reshape(-1), s_w[t0:t1].reshape(-1), h3, wg, wu, wd]
        aliases = {}
        if out is not None:
            in_specs.append(pl.BlockSpec(memory_space=pl.ANY))
            aliases = {len(operands): 0}
            operands.append(out)
        grid_spec = pltpu.PrefetchScalarGridSpec(
            num_scalar_prefetch=n_pref,
            grid=(nb,),
            in_specs=in_specs,
            out_specs=pl.BlockSpec((1, (ts + 1) * SUBLANES, LANES), tile_map, pipeline_mode=pl.Buffered(1)),
            scratch_shapes=[pltpu.VMEM((MOE_BLOCK * SUBLANES, LANES), F32),
                            pltpu.VMEM((SUBLANES * MOE_STRIDE, LANES), F32)],
        )
        out = pl.pallas_call(
            functools.partial(_moe_kernel, blocks_per_tile=bpt, dump_row=ts * SUBLANES),
            grid_spec=grid_spec,
            out_shape=jax.ShapeDtypeStruct((n_tiles, (ts + 1) * SUBLANES, LANES), F32),
            input_output_aliases=aliases,
            compiler_params=_cp(("arbitrary",), VMEM_LIMIT),
            name="routed_experts",
        )(*operands)
    return out


def _ffn_out_kernel(x_ref, h_ref, r_ref, g2_ref, sg_ref, su_ref, sd_ref, fw_ref, o_ref, *, final):
    t = x_ref.shape[0]
    h = _load_token_tiles(h_ref.at[0], t).astype(BF16)
    a = _silu(_dot(h, sg_ref[...])) * _dot(h, su_ref[...])
    y = _load_token_tiles(r_ref.at[0], t) + _dot(a.astype(BF16), sd_ref[...])
    x2 = x_ref[...] + g2_ref[0] * y
    if final:
        var = jnp.mean(x2 * x2, axis=-1, keepdims=True)
        x2 = (x2 * lax.rsqrt(var + NORM_EPS)) * fw_ref[...]
    o_ref[...] = x2


def _ffn_out(x1, h3, routed, ts, g2_rows, tiles_per_row, sg, su, sd, fw, final):
    n_tok, d = x1.shape
    t = 256
    per = ts // t
    last = g2_rows.shape[0] - 1
    tok = pl.BlockSpec((t, d), lambda i: (i, 0))
    tiles = pl.BlockSpec((1, t * SUBLANES, LANES), lambda i: (i // per, i % per, 0))
    const = lambda shape: pl.BlockSpec(shape, lambda i: (0, 0))
    return pl.pallas_call(
        functools.partial(_ffn_out_kernel, final=final),
        grid=(n_tok // t,),
        in_specs=[tok, tiles, tiles,
                  pl.BlockSpec((1, 1, d), lambda i: (jnp.minimum(i // tiles_per_row, last), 0, 0)),
                  const((d, EXPERT_DIM)), const((d, EXPERT_DIM)), const((EXPERT_DIM, d)), const((1, d))],
        out_specs=tok,
        out_shape=jax.ShapeDtypeStruct((n_tok, d), F32),
        compiler_params=_cp(("parallel",), VMEM_LIMIT),
        name="shared_expert_residual",
    )(x1, h3, routed, g2_rows, sg, su, sd, fw.reshape(1, d))


def _rope_tables(rows, dim, group, lo):
    pos_r = jnp.repeat(jnp.arange(rows, dtype=F32), GRID_W)
    pos_c = jnp.tile(jnp.arange(GRID_W, dtype=F32), rows)
    n_freq = dim // 4
    inv = ROPE_BASE ** (-jnp.arange(n_freq, dtype=F32) / n_freq)
    ang = jnp.concatenate([pos_r[:, None] * inv, pos_c[:, None] * inv], axis=-1)
    cos, sin = jnp.cos(ang), jnp.sin(ang)
    n = rows * GRID_W
    half = dim // 2
    cos_g = jnp.ones((n, group), F32).at[:, lo:lo + dim].set(jnp.concatenate([cos, cos], axis=-1))
    sin_g = jnp.zeros((n, group), F32).at[:, lo:lo + dim].set(jnp.concatenate([-sin, sin], axis=-1))
    reps = LANES // group
    return jnp.tile(cos_g, (1, reps)), jnp.tile(sin_g, (1, reps))


def _pad_in_proj(w_in):
    d = w_in.shape[0]
    body = w_in[:, :IN_COLS_PAD - HEAD_PAD]
    kr = w_in[:, IN_COLS_PAD - HEAD_PAD:]
    kr_pad = jnp.zeros((d, HEAD_PAD), w_in.dtype).at[:, MLA_NOPE_DIM:MLA_NOPE_DIM + MLA_ROPE_DIM].set(kr)
    return jnp.concatenate([body, kr_pad], axis=1).astype(BF16)


def _pad_heads(w, width):
    k = w.shape[0]
    w3 = w.reshape(k, MLA_HEADS, width)
    return jnp.zeros((k, MLA_HEADS, HEAD_PAD), w.dtype).at[:, :, :width].set(w3).reshape(
        k, MLA_HEADS * HEAD_PAD).astype(BF16)


def kernel(x, c, ctx, c_ctx, mod_w, mod_b, norm1_w, w_in, conv_w, conv_b, conv_ln_w, conv_ln_b,
           ret_decay_logit, ret_gn_w, q_norm_w, w_uq, kv_norm_w, w_ukv, w_out, norm2_w,
           router_w, router_b, exp_w_gate, exp_w_up, exp_w_down, sh_w_gate, sh_w_up, sh_w_down,
           final_norm_w):
    b, n_lat, d = x.shape
    n_ctx = ctx.shape[1]
    depth = mod_w.shape[0]
    rows = n_lat // GRID_W
    cos_ret, sin_ret = _rope_tables(rows, RET_QK_DIM, RET_QK_DIM, 0)
    cos_mla, sin_mla = _rope_tables(rows, MLA_ROPE_DIM, HEAD_PAD, MLA_NOPE_DIM)

    mod_rows = -(-(b + 1) // SUBLANES) * SUBLANES
    cc = jnp.zeros((mod_rows, d), F32).at[:b].set(c).at[b].set(c_ctx)

    n_l = b * n_lat
    wg_all, wu_all, wd_all = exp_w_gate.astype(BF16), exp_w_up.astype(BF16), exp_w_down.astype(BF16)
    xl, xl_row = x.reshape(n_l, d), 0
    xc, xc_row = ctx.reshape(b * n_ctx, d), 0
    for i in range(depth):
        last = i == depth - 1
        mod = _modulation(cc, mod_w[i], mod_b[i])
        ml = mod[:b].reshape(b, 1, 6, d)
        sh1, sc1, g1, sh2, sc2, g2 = [ml[:, :, j, :] for j in range(6)]
        mc = jnp.broadcast_to(mod[b].reshape(1, 1, 6, d), (b, 1, 6, d))
        csh1, csc1, cg1, csh2, csc2, cg2 = [mc[:, :, j, :] for j in range(6)]

        w_in_p = _pad_in_proj(w_in[i])
        ul, rl, mlat = _inproj(xl, xl_row, b, n_lat, norm1_w[i], sh1, sc1, w_in_p)
        uc, rc, mctx = _inproj(xc, xc_row, b, n_ctx, norm1_w[i], csh1, csc1, w_in_p)

        conv_l = _conv(ul, conv_w[i], conv_b[i], conv_ln_w[i], conv_ln_b[i])
        log_gamma = jax.nn.log_sigmoid(ret_decay_logit[i].astype(F32))
        ret_l, ret_c = _retention(rl, rc, log_gamma, cos_ret, sin_ret, ret_gn_w[i], not last)

        wq_p = _pad_heads(w_uq[i], MLA_NOPE_DIM + MLA_ROPE_DIM)
        wkv = w_ukv[i].reshape(MLA_KV_RANK, MLA_HEADS, MLA_NOPE_DIM + MLA_V_DIM)
        wk_p = _pad_heads(wkv[:, :, :MLA_NOPE_DIM].reshape(MLA_KV_RANK, -1), MLA_NOPE_DIM)
        wv = wkv[:, :, MLA_NOPE_DIM:].reshape(MLA_KV_RANK, -1).astype(BF16)
        ql, kl, vl = _mla_proj(mlat, q_norm_w[i], kv_norm_w[i], wq_p, wk_p, wv, cos_mla, sin_mla, True)
        qc, kc, vc = _mla_proj(mctx, q_norm_w[i], kv_norm_w[i], wq_p, wk_p, wv,
                               cos_mla[:n_ctx], sin_mla[:n_ctx], False)
        mla_l = _attention(ql, [kc, kl], [vc, vl])

        wo = w_out[i].astype(BF16)
        rw_hi = router_w[i].astype(BF16)
        rw_lo = (router_w[i] - rw_hi.astype(F32)).astype(BF16)
        n_tok = n_l if last else n_l + b * n_ctx
        outs = _outproj(conv_l, ret_l, mla_l, xl, xl_row, g1, wo, norm2_w[i], sh2, sc2,
                        rw_hi, rw_lo, router_b[i], n_tok, 0, None)
        g2_rows = g2
        if not last:
            conv_c = _conv(uc, conv_w[i], conv_b[i], conv_ln_w[i], conv_ln_b[i])
            mla_c = _attention(qc, [kc], [vc])
            outs = _outproj(conv_c, ret_c, mla_c, xc, xc_row, cg1, wo, norm2_w[i], csh2, csc2,
                            rw_hi, rw_lo, router_b[i], n_tok, n_l, outs)
            g2_rows = jnp.concatenate([g2, cg2[:1]], axis=0)
        x1, h2t, idx, wts = outs
        idx = idx[:, :TOP_K]
        wts = wts[:, :TOP_K]

        ts = _moe_tile_size(n_tok)
        h3 = h2t.reshape(n_tok // ts, ts * SUBLANES, LANES)
        routed = _routed_experts(h3, idx, wts, wg_all, wu_all, wd_all, i, ts)
        x2 = _ffn_out(x1, h3, routed, ts, g2_rows, n_lat // 256, sh_w_gate[i].astype(BF16),
                      sh_w_up[i].astype(BF16), sh_w_down[i].astype(BF16), final_norm_w, last)
        xl, xl_row = x2, 0
        xc, xc_row = x2, n_l
    return xl.reshape(b, n_lat, d)
```

```python
import functools

import jax
import jax.numpy as jnp
from jax import lax
from jax.experimental import pallas as pl
from jax.experimental.pallas import tpu as pltpu

F32 = jnp.float32
BF16 = jnp.bfloat16

D_MODEL = 1024
GRID_W = 64
CONV_CH = 256
CONV_K = 31
RET_HEADS = 4
RET_QK_DIM = 32
RET_V_DIM = 64
RET_CHUNK = 128
MLA_HEADS = 8
MLA_NOPE_DIM = 64
MLA_ROPE_DIM = 32
MLA_V_DIM = 64
MLA_Q_RANK = 256
MLA_KV_RANK = 128
ROPE_BASE = 10000.0
N_EXPERTS = 64
TOP_K = 6
EXPERT_DIM = 256
ROUTED_SCALE = 2.5
NORM_EPS = 1e-6

LANES = 128
SUBLANES = 8
HEAD_PAD = 128
RET_W = RET_HEADS * RET_QK_DIM
RET_VW = RET_HEADS * RET_V_DIM
IN_COLS_PAD = 2 * CONV_CH + 2 * RET_W + 2 * RET_VW + MLA_Q_RANK + MLA_KV_RANK + HEAD_PAD
MOE_BLOCK = 256
MOE_STRIDE = MOE_BLOCK + SUBLANES
VMEM_LIMIT = 56 * 1024 * 1024


def _cp(sem, vmem=None):
    return pltpu.CompilerParams(dimension_semantics=sem, vmem_limit_bytes=vmem)


def _dot(a, b):
    return jnp.dot(a, b, preferred_element_type=F32)


def _split_bf16(a):
    hi = a.astype(BF16)
    lo = (a - hi.astype(F32)).astype(BF16)
    return hi, lo


def _sigmoid(x):
    return 1.0 / (1.0 + jnp.exp(-x))


def _silu(x):
    return x * _sigmoid(x)


def _mod_kernel(c_ref, w_ref, b_ref, o_ref):
    a_hi, a_lo = _split_bf16(_silu(c_ref[...]))
    w_hi, w_lo = _split_bf16(w_ref[...])
    o_ref[...] = _dot(a_hi, w_hi) + _dot(a_lo, w_hi) + _dot(a_hi, w_lo) + b_ref[...]


def _modulation(cc, w, b):
    rows, d = cc.shape
    n = w.shape[1]
    bn = 1536
    return pl.pallas_call(
        _mod_kernel,
        grid=(n // bn,),
        in_specs=[pl.BlockSpec((rows, d), lambda j: (0, 0)),
                  pl.BlockSpec((d, bn), lambda j: (0, j)),
                  pl.BlockSpec((1, bn), lambda j: (0, j))],
        out_specs=pl.BlockSpec((rows, bn), lambda j: (0, j)),
        out_shape=jax.ShapeDtypeStruct((rows, n), F32),
        compiler_params=_cp(("arbitrary",), VMEM_LIMIT),
        name="modulation",
    )(cc, w, b.reshape(1, n))


def _rms_mod(x, nw, sh, sc):
    var = jnp.mean(x * x, axis=-1, keepdims=True)
    h = (x * lax.rsqrt(var + NORM_EPS)) * nw
    return h * (1.0 + sc) + sh


def _inproj_kernel(x_ref, nw_ref, sh_ref, sc_ref, w_ref, u_ref, r_ref, m_ref):
    h = _rms_mod(x_ref[...], nw_ref[...], sh_ref[0], sc_ref[0])
    z = _dot(h.astype(BF16), w_ref[...])
    c0 = 2 * CONV_CH
    c1 = c0 + 2 * RET_W + 2 * RET_VW
    u_ref[0] = z[:, :c0]
    r_ref[0] = z[:, c0:c1]
    m_ref[0] = z[:, c1:]


def _inproj(x_flat, row0, b, n, nw, sh, sc, w_pad):
    d = x_flat.shape[1]
    t = min(512, n)
    nt = n // t
    off = row0 // t
    wu, wr, wm = 2 * CONV_CH, 2 * RET_W + 2 * RET_VW, MLA_Q_RANK + MLA_KV_RANK + HEAD_PAD
    tok = lambda w: pl.BlockSpec((1, t, w), lambda i, j: (i, j, 0))
    per_b = pl.BlockSpec((1, 1, d), lambda i, j: (i, 0, 0))
    return pl.pallas_call(
        _inproj_kernel,
        grid=(b, nt),
        in_specs=[pl.BlockSpec((t, d), lambda i, j: (off + i * nt + j, 0)),
                  pl.BlockSpec((1, d), lambda i, j: (0, 0)), per_b, per_b,
                  pl.BlockSpec((d, IN_COLS_PAD), lambda i, j: (0, 0))],
        out_specs=[tok(wu), tok(wr), tok(wm)],
        out_shape=[jax.ShapeDtypeStruct((b, n, wu), F32),
                   jax.ShapeDtypeStruct((b, n, wr), F32),
                   jax.ShapeDtypeStruct((b, n, wm), F32)],
        compiler_params=_cp(("parallel", "parallel"), VMEM_LIMIT),
        name="norm1_inproj",
    )(x_flat, nw.reshape(1, d), sh, sc, w_pad)


_CONV_PAD = 16
_CONV_ROWS = 128


def _conv_kernel(u_ref, cw_ref, cb_ref, lw_ref, lb_ref, o_ref, hp_ref, *, n):
    c = CONV_CH
    hp_ref[0:_CONV_PAD, :] = jnp.zeros((_CONV_PAD, c), F32)
    hp_ref[n + _CONV_PAD:n + 2 * _CONV_PAD, :] = jnp.zeros((_CONV_PAD, c), F32)

    def glu(i, carry):
        r = pl.multiple_of(i * _CONV_ROWS, _CONV_ROWS)
        u = u_ref[0, pl.ds(r, _CONV_ROWS), :]
        hp_ref[pl.ds(r + _CONV_PAD, _CONV_ROWS), :] = u[:, :c] * _sigmoid(u[:, c:])
        return carry

    lax.fori_loop(0, n // _CONV_ROWS, glu, 0)

    def conv(i, carry):
        r = pl.multiple_of(i * _CONV_ROWS, _CONV_ROWS)
        base = _CONV_PAD - CONV_K // 2
        n_q = (base + CONV_K - 1) // SUBLANES + 1
        acc = jnp.zeros((_CONV_ROWS, c), F32)
        for s in range(SUBLANES):
            part = None
            for q in range(n_q):
                k = q * SUBLANES + s - base
                if 0 <= k < CONV_K:
                    win = hp_ref[pl.ds(r + q * SUBLANES, _CONV_ROWS + SUBLANES), :]
                    term = cw_ref[k:k + 1, :] * win
                    part = term if part is None else part + term
            acc = acc + part[s:s + _CONV_ROWS, :]
        hcv = acc + cb_ref[...]
        mu = jnp.mean(hcv, axis=-1, keepdims=True)
        dlt = hcv - mu
        var = jnp.mean(dlt * dlt, axis=-1, keepdims=True)
        y = (dlt * lax.rsqrt(var + NORM_EPS)) * lw_ref[...] + lb_ref[...]
        o_ref[0, pl.ds(r, _CONV_ROWS), :] = _silu(y)
        return carry

    lax.fori_loop(0, n // _CONV_ROWS, conv, 0)


def _conv(u, cw, cb, lw, lb):
    b, n, _ = u.shape
    c = CONV_CH
    vec = pl.BlockSpec((1, c), lambda i: (0, 0))
    return pl.pallas_call(
        functools.partial(_conv_kernel, n=n),
        grid=(b,),
        in_specs=[pl.BlockSpec((1, n, 2 * c), lambda i: (i, 0, 0)),
                  pl.BlockSpec((CONV_K, c), lambda i: (0, 0)), vec, vec, vec],
        out_specs=pl.BlockSpec((1, n, c), lambda i: (i, 0, 0)),
        out_shape=jax.ShapeDtypeStruct((b, n, c), F32),
        scratch_shapes=[pltpu.VMEM((n + 2 * _CONV_PAD, c), F32)],
        compiler_params=_cp(("parallel",), VMEM_LIMIT),
        name="conformer_conv",
    )(u, cw, cb.reshape(1, c), lw.reshape(1, c), lb.reshape(1, c))


def _rope_partner(x, group, lo):
    half = 16
    lane = lax.broadcasted_iota(jnp.int32, x.shape, 1) % group
    first = (lane >= lo) & (lane < lo + half)
    return jnp.where(first, pltpu.roll(x, LANES - half, 1), pltpu.roll(x, half, 1))


def _ret_kernel(lg_ref, rl_ref, rc_ref, cos_ref, sin_ref, gn_ref, avg_ref, ol_ref, oc_ref,
                q_s, k_s, yl_s, yc_s, dst_s, dq_s, dk_s, dch_s, *, n_lat, n_ctx, ctx_out):
    ch = RET_CHUNK
    lane_q = lax.broadcasted_iota(jnp.int32, (1, RET_W), 1) // RET_QK_DIM
    lane_v = lax.broadcasted_iota(jnp.int32, (1, RET_VW), 1) // RET_V_DIM
    row_h = lax.broadcasted_iota(jnp.int32, (RET_W, 1), 0) // RET_QK_DIM
    bd = (row_h == lane_v).astype(F32)
    qmask = [(lane_q == h).astype(F32) for h in range(RET_HEADS)]
    vmask = [(lane_v == h).astype(F32) for h in range(RET_HEADS)]

    ri = lax.broadcasted_iota(jnp.int32, (ch, ch), 0).astype(F32)
    ci = lax.broadcasted_iota(jnp.int32, (ch, ch), 1).astype(F32)
    rowi = lax.broadcasted_iota(jnp.int32, (ch, 1), 0).astype(F32)
    for d in range(2):
        lg_lane = jnp.zeros((1, RET_W), F32)
        lg_row = jnp.zeros((RET_W, 1), F32)
        for h in range(RET_HEADS):
            lg = lg_ref[d * RET_HEADS + h]
            lg_lane = jnp.where(lane_q == h, lg, lg_lane)
            lg_row = jnp.where(row_h == h, lg, lg_row)
            rel = (ri - ci) if d == 0 else (ci - ri)
            dst_s[d, h * ch:(h + 1) * ch, :] = jnp.where(
                rel >= 0, jnp.exp(lg * jnp.maximum(rel, 0.0)), 0.0)
        if d == 0:
            dq_s[d] = jnp.exp(lg_lane * (rowi + 1.0))
            dk_s[d] = jnp.exp(lg_lane * (ch - 1.0 - rowi))
        else:
            dq_s[d] = jnp.exp(lg_lane * (ch - rowi))
            dk_s[d] = jnp.exp(lg_lane * rowi)
        dch_s[d] = jnp.exp(lg_row * float(ch)) * jnp.ones((1, RET_VW), F32)

    kscale = RET_QK_DIM ** -0.5

    def stage(src_ref, n, rope):
        def body(i, carry):
            r = pl.multiple_of(i * ch, ch)
            q = src_ref[0, pl.ds(r, ch), 0:RET_W]
            k = src_ref[0, pl.ds(r, ch), RET_W:2 * RET_W] * kscale
            if rope:
                cs = cos_ref[pl.ds(r, ch), :]
                sn = sin_ref[pl.ds(r, ch), :]
                q = q * cs + _rope_partner(q, RET_QK_DIM, 0) * sn
                k = k * cs + _rope_partner(k, RET_QK_DIM, 0) * sn
            q_s[pl.ds(r, ch), :] = q
            k_s[pl.ds(r, ch), :] = k
            return carry
        lax.fori_loop(0, n // ch, body, 0)

    def scan(src_ref, y_ref, n, s0_fwd, s0_bwd):
        nchunks = n // ch

        def one(d, c, s):
            r = pl.multiple_of(c * ch, ch)
            qc = q_s[pl.ds(r, ch), :]
            kc = k_s[pl.ds(r, ch), :]
            vc = src_ref[0, pl.ds(r, ch), 2 * RET_W:2 * RET_W + RET_VW].astype(BF16)
            qst = jnp.concatenate([qc * qmask[h] for h in range(RET_HEADS)], axis=0).astype(BF16)
            inner = lax.dot_general(qst, kc.astype(BF16), (((1,), (1,)), ((), ())),
                                    preferred_element_type=F32) * dst_s[d]
            o = _dot(inner.astype(BF16), vc)
            y = _dot((qc * dq_s[d]).astype(BF16), s.astype(BF16))
            for h in range(RET_HEADS):
                y = y + o[h * ch:(h + 1) * ch, :] * vmask[h]
            kd_t = (kc * dk_s[d]).T.astype(BF16)
            s_new = s * dch_s[d] + _dot(kd_t, vc) * bd
            y_ref[d, pl.ds(r, ch), :] = y
            return s_new

        def body(i, carry):
            return one(0, i, carry[0]), one(1, nchunks - 1 - i, carry[1])

        return lax.fori_loop(0, nchunks, body, (s0_fwd, s0_bwd))

    def finish(src_ref, y_ref, out_ref, n):
        def body(i, carry):
            r = pl.multiple_of(i * ch, ch)
            y = y_ref[0, pl.ds(r, ch), :] + y_ref[1, pl.ds(r, ch), :]
            y_hi, y_lo = _split_bf16(y)
            mu = _dot(y_hi, avg_ref[...]) + _dot(y_lo, avg_ref[...])
            dlt = y - mu
            d_hi, d_lo = _split_bf16(dlt * dlt)
            var = _dot(d_hi, avg_ref[...]) + _dot(d_lo, avg_ref[...])
            yn = dlt * lax.rsqrt(var + NORM_EPS)
            g = src_ref[0, pl.ds(r, ch), 2 * RET_W + RET_VW:2 * RET_W + 2 * RET_VW]
            out_ref[0, pl.ds(r, ch), :] = _silu(g) * (yn * gn_ref[...])
            return carry
        lax.fori_loop(0, n // ch, body, 0)

    s_zero = jnp.zeros((RET_W, RET_VW), F32)
    stage(rc_ref, n_ctx, False)
    sc_f, sc_b = scan(rc_ref, yc_s, n_ctx, s_zero, s_zero)
    if ctx_out:
        finish(rc_ref, yc_s, oc_ref, n_ctx)
    else:
        oc_ref[...] = jnp.zeros(oc_ref.shape, F32)
    stage(rl_ref, n_lat, True)
    scan(rl_ref, yl_s, n_lat, sc_f, sc_b)
    finish(rl_ref, yl_s, ol_ref, n_lat)


def _retention(r_lat, r_ctx, log_gamma, cos_t, sin_t, gn_w, ctx_out):
    b, n_lat, w = r_lat.shape
    n_ctx = r_ctx.shape[1]
    grp = jnp.arange(RET_VW) // RET_V_DIM
    avg = ((grp[:, None] == grp[None, :]).astype(F32) / RET_V_DIM).astype(BF16)
    tab = pl.BlockSpec((n_lat, RET_W), lambda i, lg: (0, 0))
    grid_spec = pltpu.PrefetchScalarGridSpec(
        num_scalar_prefetch=1,
        grid=(b,),
        in_specs=[pl.BlockSpec((1, n_lat, w), lambda i, lg: (i, 0, 0)),
                  pl.BlockSpec((1, n_ctx, w), lambda i, lg: (i, 0, 0)),
                  tab, tab,
                  pl.BlockSpec((1, RET_VW), lambda i, lg: (0, 0)),
                  pl.BlockSpec((RET_VW, RET_VW), lambda i, lg: (0, 0))],
        out_specs=[pl.BlockSpec((1, n_lat, RET_VW), lambda i, lg: (i, 0, 0)),
                   pl.BlockSpec((1, n_ctx, RET_VW), lambda i, lg: (i, 0, 0))],
        scratch_shapes=[pltpu.VMEM((max(n_lat, n_ctx), RET_W), F32),
                        pltpu.VMEM((max(n_lat, n_ctx), RET_W), F32),
                        pltpu.VMEM((2, n_lat, RET_VW), F32),
                        pltpu.VMEM((2, n_ctx, RET_VW), F32),
                        pltpu.VMEM((2, RET_HEADS * RET_CHUNK, RET_CHUNK), F32),
                        pltpu.VMEM((2, RET_CHUNK, RET_W), F32),
                        pltpu.VMEM((2, RET_CHUNK, RET_W), F32),
                        pltpu.VMEM((2, RET_W, RET_VW), F32)],
    )
    return pl.pallas_call(
        functools.partial(_ret_kernel, n_lat=n_lat, n_ctx=n_ctx, ctx_out=ctx_out),
        grid_spec=grid_spec,
        out_shape=[jax.ShapeDtypeStruct((b, n_lat, RET_VW), F32),
                   jax.ShapeDtypeStruct((b, n_ctx, RET_VW), F32)],
        compiler_params=_cp(("parallel",), VMEM_LIMIT),
        name="retention",
    )(log_gamma.reshape(-1), r_lat, r_ctx, cos_t, sin_t, gn_w.reshape(1, RET_VW), avg)


def _mla_proj_kernel(m_ref, qn_ref, kvn_ref, wq_ref, wk_ref, wv_ref, cos_ref, sin_ref,
                     q_ref, k_ref, v_ref, *, rope):
    m = m_ref[0]
    cq = m[:, :MLA_Q_RANK]
    ckv = m[:, MLA_Q_RANK:MLA_Q_RANK + MLA_KV_RANK]
    kr = m[:, MLA_Q_RANK + MLA_KV_RANK:]

    def rms(x, w):
        var = jnp.mean(x * x, axis=-1, keepdims=True)
        return (x * lax.rsqrt(var + NORM_EPS)) * w

    q = _dot(rms(cq, qn_ref[...]).astype(BF16), wq_ref[...])
    ckv_n = rms(ckv, kvn_ref[...]).astype(BF16)
    k = _dot(ckv_n, wk_ref[...])
    v_ref[0] = _dot(ckv_n, wv_ref[...]).astype(BF16)
    if rope:
        cs = cos_ref[...]
        sn = sin_ref[...]
        kr = kr * cs + _rope_partner(kr, HEAD_PAD, MLA_NOPE_DIM) * sn
    scale = (MLA_NOPE_DIM + MLA_ROPE_DIM) ** -0.5
    for h in range(MLA_HEADS):
        sl = slice(h * HEAD_PAD, (h + 1) * HEAD_PAD)
        qh = q[:, sl]
        if rope:
            qh = qh * cs + _rope_partner(qh, HEAD_PAD, MLA_NOPE_DIM) * sn
        q_ref[0, :, sl] = (qh * scale).astype(BF16)
        k_ref[0, :, sl] = (k[:, sl] + kr).astype(BF16)


def _mla_proj(m, qn_w, kvn_w, wq_pad, wk_pad, wv, cos_t, sin_t, rope):
    b, n, w = m.shape
    t = min(512, n)
    hw = MLA_HEADS * HEAD_PAD
    vw = MLA_HEADS * MLA_V_DIM
    const = lambda shape: pl.BlockSpec(shape, lambda i, j: (0, 0))
    tab = pl.BlockSpec((t, HEAD_PAD), lambda i, j: (j, 0))
    return pl.pallas_call(
        functools.partial(_mla_proj_kernel, rope=rope),
        grid=(b, n // t),
        in_specs=[pl.BlockSpec((1, t, w), lambda i, j: (i, j, 0)),
                  const((1, MLA_Q_RANK)), const((1, MLA_KV_RANK)),
                  const((MLA_Q_RANK, hw)), const((MLA_KV_RANK, hw)), const((MLA_KV_RANK, vw)),
                  tab, tab],
        out_specs=[pl.BlockSpec((1, t, hw), lambda i, j: (i, j, 0)),
                   pl.BlockSpec((1, t, hw), lambda i, j: (i, j, 0)),
                   pl.BlockSpec((1, t, vw), lambda i, j: (i, j, 0))],
        out_shape=[jax.ShapeDtypeStruct((b, n, hw), BF16),
                   jax.ShapeDtypeStruct((b, n, hw), BF16),
                   jax.ShapeDtypeStruct((b, n, vw), BF16)],
        compiler_params=_cp(("parallel", "parallel"), VMEM_LIMIT),
        name="mla_proj",
    )(m, qn_w.reshape(1, -1), kvn_w.reshape(1, -1), wq_pad, wk_pad, wv, cos_t, sin_t)


def _attn_kernel(*refs, n_seg):
    q_ref = refs[0]
    k_refs = refs[1:1 + n_seg]
    v_refs = refs[1 + n_seg:1 + 2 * n_seg]
    o_ref = refs[1 + 2 * n_seg]
    tq = q_ref.shape[1]
    lane = lax.broadcasted_iota(jnp.int32, (tq, 2 * MLA_V_DIM), 1)
    for pair in range(MLA_HEADS // 2):
        outs = []
        for h in (2 * pair, 2 * pair + 1):
            sl = slice(h * HEAD_PAD, (h + 1) * HEAD_PAD)
            qh = q_ref[0, :, sl]
            s = [lax.dot_general(qh, kr[0, :, sl], (((1,), (1,)), ((), ())),
                                 preferred_element_type=F32) for kr in k_refs]
            mx = s[0].max(axis=-1, keepdims=True)
            for si in s[1:]:
                mx = jnp.maximum(mx, si.max(axis=-1, keepdims=True))
            den = jnp.zeros((tq, 1), F32)
            acc = jnp.zeros((tq, 2 * MLA_V_DIM), F32)
            for si, vr in zip(s, v_refs):
                p = jnp.exp(si - mx)
                den = den + p.sum(axis=-1, keepdims=True)
                acc = acc + _dot(p.astype(BF16), vr[0, :, pair * 2 * MLA_V_DIM:(pair + 1) * 2 * MLA_V_DIM])
            outs.append(acc * (1.0 / den))
        o_ref[0, :, pair * 2 * MLA_V_DIM:(pair + 1) * 2 * MLA_V_DIM] = jnp.where(
            lane < MLA_V_DIM, outs[0], outs[1])


def _attention(q, ks, vs):
    b, nq, hw = q.shape
    tq = min(512, nq)
    vw = MLA_HEADS * MLA_V_DIM
    n_seg = len(ks)
    seg_spec = lambda a: pl.BlockSpec((1,) + a.shape[1:], lambda i, j: (i, 0, 0))
    return pl.pallas_call(
        functools.partial(_attn_kernel, n_seg=n_seg),
        grid=(b, nq // tq),
        in_specs=[pl.BlockSpec((1, tq, hw), lambda i, j: (i, j, 0))]
                 + [seg_spec(a) for a in ks] + [seg_spec(a) for a in vs],
        out_specs=pl.BlockSpec((1, tq, vw), lambda i, j: (i, j, 0)),
        out_shape=jax.ShapeDtypeStruct((b, nq, vw), F32),
        compiler_params=_cp(("parallel", "parallel"), VMEM_LIMIT),
        name="mla_attention",
    )(q, *ks, *vs)


def _store_token_tiles(ref, val):
    t = val.shape[0]
    for j in range(SUBLANES):
        ref[pl.ds(j, t, stride=SUBLANES), :] = val[:, j * LANES:(j + 1) * LANES]


def _load_token_tiles(ref, t):
    return jnp.concatenate([ref[pl.ds(j, t, stride=SUBLANES), :] for j in range(SUBLANES)], axis=1)


def _outproj_kernel(cv_ref, rt_ref, ml_ref, x_ref, g1_ref, wo_ref, nw_ref, sh_ref, sc_ref,
                    rwh_ref, rwl_ref, rb_ref, *rest):
    x1_ref, h2_ref, idx_ref, wt_ref = rest[-4:]
    c0, c1 = CONV_CH, CONV_CH + RET_VW
    y = (_dot(cv_ref[0].astype(BF16), wo_ref[:c0, :])
         + _dot(rt_ref[0].astype(BF16), wo_ref[c0:c1, :])
         + _dot(ml_ref[0].astype(BF16), wo_ref[c1:, :]))
    x1 = x_ref[...] + g1_ref[0] * y
    x1_ref[...] = x1
    h2 = _rms_mod(x1, nw_ref[...], sh_ref[0], sc_ref[0])
    _store_token_tiles(h2_ref, h2)
    h_hi, h_lo = _split_bf16(h2)
    nt_dot = lambda a, bm: lax.dot_general(a, bm, (((1,), (1,)), ((), ())), preferred_element_type=F32)
    logits = nt_dot(rwh_ref[...], h_hi) + nt_dot(rwh_ref[...], h_lo) + nt_dot(rwl_ref[...], h_hi)
    scores = _sigmoid(logits)
    sel = scores + rb_ref[...]
    t = scores.shape[1]
    eio = lax.broadcasted_iota(jnp.int32, (N_EXPERTS, t), 0).astype(F32)
    slot = lax.broadcasted_iota(jnp.int32, (SUBLANES, t), 0)
    idx_out = jnp.zeros((SUBLANES, t), jnp.int32)
    wt_out = jnp.zeros((SUBLANES, t), F32)
    wsum = jnp.zeros((1, t), F32)
    for k in range(TOP_K):
        mx = jnp.max(sel, axis=0, keepdims=True)
        ik = jnp.min(jnp.where(sel == mx, eio, float(N_EXPERTS)), axis=0, keepdims=True)
        hit = eio == ik
        wk = jnp.sum(jnp.where(hit, scores, 0.0), axis=0, keepdims=True)
        sel = jnp.where(hit, -jnp.inf, sel)
        idx_out = jnp.where(slot == k, ik.astype(jnp.int32), idx_out)
        wt_out = jnp.where(slot == k, wk, wt_out)
        wsum = wsum + wk
    idx_ref[...] = idx_out
    wt_ref[...] = wt_out / wsum * ROUTED_SCALE


def _outproj(conv, ret, mla, x_flat, x_row0, g1, wo, nw, sh, sc, rw_hi, rw_lo, rb, n_total, row0, carry):
    b, n, _ = conv.shape
    d = x_flat.shape[1]
    x = x_flat
    t = min(256, n)
    nt = n // t
    off = row0 // t
    x_off = x_row0 // t
    tok = lambda w: pl.BlockSpec((1, t, w), lambda i, j: (i, j, 0))
    per_b = pl.BlockSpec((1, 1, d), lambda i, j: (i, 0, 0))
    const = lambda shape: pl.BlockSpec(shape, lambda i, j: (0, 0))
    flat = lambda rows, w: pl.BlockSpec((rows, w), lambda i, j: (off + i * nt + j, 0))
    x_spec = pl.BlockSpec((t, d), lambda i, j: (x_off + i * nt + j, 0))
    in_specs = [tok(CONV_CH), tok(RET_VW), tok(MLA_HEADS * MLA_V_DIM), x_spec, per_b,
                const((d, d)), const((1, d)), per_b, per_b,
                const((N_EXPERTS, d)), const((N_EXPERTS, d)), const((N_EXPERTS, 1))]
    operands = [conv, ret, mla, x, g1, wo, nw.reshape(1, d), sh, sc, rw_hi, rw_lo, rb.reshape(N_EXPERTS, 1)]
    choice = pl.BlockSpec((SUBLANES, t), lambda i, j: (0, off + i * nt + j))
    aliases = {}
    if carry is not None:
        aliases = {len(operands) + k: k for k in range(len(carry))}
        in_specs += [pl.BlockSpec(memory_space=pl.ANY)] * len(carry)
        operands += list(carry)
    return pl.pallas_call(
        _outproj_kernel,
        grid=(b, nt),
        in_specs=in_specs,
        out_specs=[flat(t, d), flat(t * SUBLANES, LANES), choice, choice],
        out_shape=[jax.ShapeDtypeStruct((n_total, d), F32),
                   jax.ShapeDtypeStruct((n_total * SUBLANES, LANES), F32),
                   jax.ShapeDtypeStruct((SUBLANES, n_total), jnp.int32),
                   jax.ShapeDtypeStruct((SUBLANES, n_total), F32)],
        input_output_aliases=aliases,
        compiler_params=_cp(("parallel", "parallel"), VMEM_LIMIT),
        name="outproj_norm2_router",
    )(*operands)


_MOE_ROWS = 32
_MOE_RMW = 16


def _moe_kernel(be_ref, src_ref, cnt_ref, tok_ref, wt_ref, h_ref, wg_ref, wu_ref, wd_ref, *rest,
                blocks_per_tile, dump_row):
    o_ref, xt_ref, yt_ref = rest[-3:]
    b = pl.program_id(0)

    @pl.when(b == 0)
    def _():
        xt_ref[...] = jnp.zeros(xt_ref.shape, F32)

    @pl.when(b % blocks_per_tile == 0)
    def _():
        o_ref[...] = jnp.zeros(o_ref.shape, F32)

    cnt = cnt_ref[b]

    @pl.when(cnt > 0)
    def _():
        base = src_ref[b]
        last = cnt - 1
        def gather_rows(g0, partial):
            for mi in range(g0, g0 + _MOE_ROWS):
                i = base + (jnp.minimum(mi, last) if partial else mi)
                t8 = pl.multiple_of(tok_ref[i], SUBLANES)
                xt_ref[mi * SUBLANES:(mi + 1) * SUBLANES, :] = h_ref[0, pl.ds(t8, SUBLANES), :]

        for g0 in range(0, MOE_BLOCK, _MOE_ROWS):
            pl.when(g0 + _MOE_ROWS <= cnt)(functools.partial(gather_rows, g0, False))
            pl.when((g0 < cnt) & (g0 + _MOE_ROWS > cnt))(functools.partial(gather_rows, g0, True))
        x = jnp.concatenate([xt_ref[pl.ds(j, MOE_BLOCK, stride=SUBLANES), :] for j in range(SUBLANES)],
                            axis=1).astype(BF16)
        g = _dot(x, wg_ref[...])
        u = _dot(x, wu_ref[...])
        y = _dot((_silu(g) * u).astype(BF16), wd_ref[...])
        for j in range(SUBLANES):
            yt_ref[pl.ds(j * MOE_STRIDE, MOE_BLOCK), :] = y[:, j * LANES:(j + 1) * LANES]
        def scatter_rows(g0, partial):
            for m0 in range(g0, g0 + _MOE_ROWS, _MOE_RMW):
                rows = []
                for mi in range(m0, m0 + _MOE_RMW):
                    if partial:
                        i = base + jnp.minimum(mi, last)
                        t8 = pl.multiple_of(jnp.where(mi < cnt, tok_ref[i], dump_row), SUBLANES)
                    else:
                        i = base + mi
                        t8 = pl.multiple_of(tok_ref[i], SUBLANES)
                    upd = yt_ref[pl.ds(mi, SUBLANES, stride=MOE_STRIDE), :] * wt_ref[i]
                    rows.append((t8, o_ref[0, pl.ds(t8, SUBLANES), :] + upd))
                for t8, val in rows:
                    o_ref[0, pl.ds(t8, SUBLANES), :] = val

        for g0 in range(0, MOE_BLOCK, _MOE_ROWS):
            pl.when(g0 + _MOE_ROWS <= cnt)(functools.partial(scatter_rows, g0, False))
            pl.when((g0 < cnt) & (g0 + _MOE_ROWS > cnt))(functools.partial(scatter_rows, g0, True))


def _moe_tile_size(n_tok):
    for s in (4096, 2048, 1024, 512, 256):
        if n_tok % s == 0:
            return s
    raise ValueError(f"token count {n_tok} must be a multiple of 256")


_MOE_SMEM_WORDS = 64 * 1024


def _routed_experts(h3, idx, wts, wg, wu, wd, layer, ts):
    n_tiles = h3.shape[0]
    d = SUBLANES * LANES
    n_assign = ts * TOP_K
    bpt = n_assign // MOE_BLOCK + N_EXPERTS

    e_t = idx.reshape(n_tiles, n_assign).astype(jnp.int32)
    tok8 = (jnp.arange(n_assign, dtype=jnp.int32) // TOP_K) * SUBLANES
    tok_bits = (ts * SUBLANES - 1).bit_length()
    s_key, s_w = lax.sort((e_t * (1 << tok_bits) + tok8, wts.reshape(n_tiles, n_assign)), dimension=1,
                          num_keys=1, is_stable=False)
    s_tok = s_key & ((1 << tok_bits) - 1)
    counts = jnp.sum((e_t[:, :, None] == jnp.arange(N_EXPERTS, dtype=jnp.int32)).astype(jnp.int32), axis=1)
    start = jnp.cumsum(counts, axis=1) - counts
    nblk = (counts + MOE_BLOCK - 1) // MOE_BLOCK
    blk_end = jnp.cumsum(nblk, axis=1)
    blk_start = blk_end - nblk
    used = blk_end[:, -1:]
    bi = jnp.broadcast_to(jnp.arange(bpt, dtype=jnp.int32), (n_tiles, bpt))
    bi_c = jnp.minimum(bi, used - 1)
    e_b = jnp.sum((blk_end[:, None, :] <= bi_c[:, :, None]).astype(jnp.int32), axis=2)
    take = lambda a: jnp.take_along_axis(a, e_b, axis=1)
    j = bi_c - take(blk_start)
    src = take(start) + j * MOE_BLOCK
    cnt = jnp.where(bi < used, jnp.clip(take(counts) - j * MOE_BLOCK, 0, MOE_BLOCK), 0)

    n_calls = -(-n_tiles // (_MOE_SMEM_WORDS // n_assign))
    bounds = [n_tiles * c // n_calls for c in range(n_calls + 1)]
    out = None
    n_pref = 5
    for c in range(n_calls):
        t0, t1 = bounds[c], bounds[c + 1]
        group = t1 - t0
        nb = group * bpt
        src_abs = src[t0:t1] + jnp.arange(group, dtype=jnp.int32)[:, None] * n_assign
        weight = lambda shape: pl.BlockSpec((None, None) + shape,
                                            lambda i, be, sr, cn, tk, wt: (layer, be[i], 0, 0))
        tile_map = lambda i, be, sr, cn, tk, wt, t0=t0: (t0 + i // bpt, 0, 0)
        in_specs = [pl.BlockSpec((1, ts * SUBLANES, LANES), tile_map, pipeline_mode=pl.Buffered(1)),
                    weight((d, EXPERT_DIM)), weight((d, EXPERT_DIM)), weight((EXPERT_DIM, d))]
        operands = [e_b[t0:t1].reshape(-1), src_abs.reshape(-1), cnt[t0:t1].reshape(-1),
                    s_tok[t0:t1].reshape(-1), s_w[t0:t1].reshape(-1), h3, wg, wu, wd]
        aliases = {}
        if out is not None:
            in_specs.append(pl.BlockSpec(memory_space=pl.ANY))
            aliases = {len(operands): 0}
            operands.append(out)
        grid_spec = pltpu.PrefetchScalarGridSpec(
            num_scalar_prefetch=n_pref,
            grid=(nb,),
            in_specs=in_specs,
            out_specs=pl.BlockSpec((1, (ts + 1) * SUBLANES, LANES), tile_map, pipeline_mode=pl.Buffered(1)),
            scratch_shapes=[pltpu.VMEM((MOE_BLOCK * SUBLANES, LANES), F32),
                            pltpu.VMEM((SUBLANES * MOE_STRIDE, LANES), F32)],
        )
        out = pl.pallas_call(
            functools.partial(_moe_kernel, blocks_per_tile=bpt, dump_row=ts * SUBLANES),
            grid_spec=grid_spec,
            out_shape=jax.ShapeDtypeStruct((n_tiles, (ts + 1) * SUBLANES, LANES), F32),
            input_output_aliases=aliases,
            compiler_params=_cp(("arbitrary",), VMEM_LIMIT),
            name="routed_experts",
        )(*operands)
    return out


def _ffn_out_kernel(x_ref, h_ref, r_ref, g2_ref, sg_ref, su_ref, sd_ref, fw_ref, o_ref, *, final):
    t = x_ref.shape[0]
    h = _load_token_tiles(h_ref.at[0], t).astype(BF16)
    a = _silu(_dot(h, sg_ref[...])) * _dot(h, su_ref[...])
    y = _load_token_tiles(r_ref.at[0], t) + _dot(a.astype(BF16), sd_ref[...])
    x2 = x_ref[...] + g2_ref[0] * y
    if final:
        var = jnp.mean(x2 * x2, axis=-1, keepdims=True)
        x2 = (x2 * lax.rsqrt(var + NORM_EPS)) * fw_ref[...]
    o_ref[...] = x2


def _ffn_out(x1, h3, routed, ts, g2_rows, tiles_per_row, sg, su, sd, fw, final):
    n_tok, d = x1.shape
    t = 256
    per = ts // t
    last = g2_rows.shape[0] - 1
    tok = pl.BlockSpec((t, d), lambda i: (i, 0))
    tiles = pl.BlockSpec((1, t * SUBLANES, LANES), lambda i: (i // per, i % per, 0))
    const = lambda shape: pl.BlockSpec(shape, lambda i: (0, 0))
    return pl.pallas_call(
        functools.partial(_ffn_out_kernel, final=final),
        grid=(n_tok // t,),
        in_specs=[tok, tiles, tiles,
                  pl.BlockSpec((1, 1, d), lambda i: (jnp.minimum(i // tiles_per_row, last), 0, 0)),
                  const((d, EXPERT_DIM)), const((d, EXPERT_DIM)), const((EXPERT_DIM, d)), const((1, d))],
        out_specs=tok,
        out_shape=jax.ShapeDtypeStruct((n_tok, d), F32),
        compiler_params=_cp(("parallel",), VMEM_LIMIT),
        name="shared_expert_residual",
    )(x1, h3, routed, g2_rows, sg, su, sd, fw.reshape(1, d))


def _rope_tables(rows, dim, group, lo):
    pos_r = jnp.repeat(jnp.arange(rows, dtype=F32), GRID_W)
    pos_c = jnp.tile(jnp.arange(GRID_W, dtype=F32), rows)
    n_freq = dim // 4
    inv = ROPE_BASE ** (-jnp.arange(n_freq, dtype=F32) / n_freq)
    ang = jnp.concatenate([pos_r[:, None] * inv, pos_c[:, None] * inv], axis=-1)
    cos, sin = jnp.cos(ang), jnp.sin(ang)
    n = rows * GRID_W
    half = dim // 2
    cos_g = jnp.ones((n, group), F32).at[:, lo:lo + dim].set(jnp.concatenate([cos, cos], axis=-1))
    sin_g = jnp.zeros((n, group), F32).at[:, lo:lo + dim].set(jnp.concatenate([-sin, sin], axis=-1))
    reps = LANES // group
    return jnp.tile(cos_g, (1, reps)), jnp.tile(sin_g, (1, reps))


def _pad_in_proj(w_in):
    d = w_in.shape[0]
    body = w_in[:, :IN_COLS_PAD - HEAD_PAD]
    kr = w_in[:, IN_COLS_PAD - HEAD_PAD:]
    kr_pad = jnp.zeros((d, HEAD_PAD), w_in.dtype).at[:, MLA_NOPE_DIM:MLA_NOPE_DIM + MLA_ROPE_DIM].set(kr)
    return jnp.concatenate([body, kr_pad], axis=1).astype(BF16)


def _pad_heads(w, width):
    k = w.shape[0]
    w3 = w.reshape(k, MLA_HEADS, width)
    return jnp.zeros((k, MLA_HEADS, HEAD_PAD), w.dtype).at[:, :, :width].set(w3).reshape(
        k, MLA_HEADS * HEAD_PAD).astype(BF16)


def kernel(x, c, ctx, c_ctx, mod_w, mod_b, norm1_w, w_in, conv_w, conv_b, conv_ln_w, conv_ln_b,
           ret_decay_logit, ret_gn_w, q_norm_w, w_uq, kv_norm_w, w_ukv, w_out, norm2_w,
           router_w, router_b, exp_w_gate, exp_w_up, exp_w_down, sh_w_gate, sh_w_up, sh_w_down,
           final_norm_w):
    b, n_lat, d = x.shape
    n_ctx = ctx.shape[1]
    depth = mod_w.shape[0]
    rows = n_lat // GRID_W
    cos_ret, sin_ret = _rope_tables(rows, RET_QK_DIM, RET_QK_DIM, 0)
    cos_mla, sin_mla = _rope_tables(rows, MLA_ROPE_DIM, HEAD_PAD, MLA_NOPE_DIM)

    mod_rows = -(-(b + 1) // SUBLANES) * SUBLANES
    cc = jnp.zeros((mod_rows, d), F32).at[:b].set(c).at[b].set(c_ctx)

    n_l = b * n_lat
    wg_all, wu_all, wd_all = exp_w_gate.astype(BF16), exp_w_up.astype(BF16), exp_w_down.astype(BF16)
    xl, xl_row = x.reshape(n_l, d), 0
    xc, xc_row = ctx.reshape(b * n_ctx, d), 0
    for i in range(depth):
        last = i == depth - 1
        mod = _modulation(cc, mod_w[i], mod_b[i])
        ml = mod[:b].reshape(b, 1, 6, d)
        sh1, sc1, g1, sh2, sc2, g2 = [ml[:, :, j, :] for j in range(6)]
        mc = jnp.broadcast_to(mod[b].reshape(1, 1, 6, d), (b, 1, 6, d))
        csh1, csc1, cg1, csh2, csc2, cg2 = [mc[:, :, j, :] for j in range(6)]

        w_in_p = _pad_in_proj(w_in[i])
        ul, rl, mlat = _inproj(xl, xl_row, b, n_lat, norm1_w[i], sh1, sc1, w_in_p)
        uc, rc, mctx = _inproj(xc, xc_row, b, n_ctx, norm1_w[i], csh1, csc1, w_in_p)

        conv_l = _conv(ul, conv_w[i], conv_b[i], conv_ln_w[i], conv_ln_b[i])
        log_gamma = jax.nn.log_sigmoid(ret_decay_logit[i].astype(F32))
        ret_l, ret_c = _retention(rl, rc, log_gamma, cos_ret, sin_ret, ret_gn_w[i], not last)

        wq_p = _pad_heads(w_uq[i], MLA_NOPE_DIM + MLA_ROPE_DIM)
        wkv = w_ukv[i].reshape(MLA_KV_RANK, MLA_HEADS, MLA_NOPE_DIM + MLA_V_DIM)
        wk_p = _pad_heads(wkv[:, :, :MLA_NOPE_DIM].reshape(MLA_KV_RANK, -1), MLA_NOPE_DIM)
        wv = wkv[:, :, MLA_NOPE_DIM:].reshape(MLA_KV_RANK, -1).astype(BF16)
        ql, kl, vl = _mla_proj(mlat, q_norm_w[i], kv_norm_w[i], wq_p, wk_p, wv, cos_mla, sin_mla, True)
        qc, kc, vc = _mla_proj(mctx, q_norm_w[i], kv_norm_w[i], wq_p, wk_p, wv,
                               cos_mla[:n_ctx], sin_mla[:n_ctx], False)
        mla_l = _attention(ql, [kc, kl], [vc, vl])

        wo = w_out[i].astype(BF16)
        rw_t = router_w[i].T
        rw_hi = rw_t.astype(BF16)
        rw_lo = (rw_t - rw_hi.astype(F32)).astype(BF16)
        n_tok = n_l if last else n_l + b * n_ctx
        outs = _outproj(conv_l, ret_l, mla_l, xl, xl_row, g1, wo, norm2_w[i], sh2, sc2,
                        rw_hi, rw_lo, router_b[i], n_tok, 0, None)
        g2_rows = g2
        if not last:
            conv_c = _conv(uc, conv_w[i], conv_b[i], conv_ln_w[i], conv_ln_b[i])
            mla_c = _attention(qc, [kc], [vc])
            outs = _outproj(conv_c, ret_c, mla_c, xc, xc_row, cg1, wo, norm2_w[i], csh2, csc2,
                            rw_hi, rw_lo, router_b[i], n_tok, n_l, outs)
            g2_rows = jnp.concatenate([g2, cg2[:1]], axis=0)
        x1, h2t, idx, wts = outs
        idx = idx[:TOP_K].T
        wts = wts[:TOP_K].T

        ts = _moe_tile_size(n_tok)
        h3 = h2t.reshape(n_tok // ts, ts * SUBLANES, LANES)
        routed = _routed_experts(h3, idx, wts, wg_all, wu_all, wd_all, i, ts)
        x2 = _ffn_out(x1, h3, routed, ts, g2_rows, n_lat // 256, sh_w_gate[i].astype(BF16),
                      sh_w_up[i].astype(BF16), sh_w_down[i].astype(BF16), final_norm_w, last)
        xl, xl_row = x2, 0
        xc, xc_row = x2, n_l
    return xl.reshape(b, n_lat, d)
```

```python
import functools

import jax
import jax.numpy as jnp
from jax import lax
from jax.experimental import pallas as pl
from jax.experimental.pallas import tpu as pltpu

F32 = jnp.float32
BF16 = jnp.bfloat16

D_MODEL = 1024
GRID_W = 64
CONV_CH = 256
CONV_K = 31
RET_HEADS = 4
RET_QK_DIM = 32
RET_V_DIM = 64
RET_CHUNK = 128
MLA_HEADS = 8
MLA_NOPE_DIM = 64
MLA_ROPE_DIM = 32
MLA_V_DIM = 64
MLA_Q_RANK = 256
MLA_KV_RANK = 128
ROPE_BASE = 10000.0
N_EXPERTS = 64
TOP_K = 6
EXPERT_DIM = 256
ROUTED_SCALE = 2.5
NORM_EPS = 1e-6

LANES = 128
SUBLANES = 8
HEAD_PAD = 128
RET_W = RET_HEADS * RET_QK_DIM
RET_VW = RET_HEADS * RET_V_DIM
IN_COLS_PAD = 2 * CONV_CH + 2 * RET_W + 2 * RET_VW + MLA_Q_RANK + MLA_KV_RANK + HEAD_PAD
MOE_BLOCK = 256
MOE_STRIDE = MOE_BLOCK + SUBLANES
VMEM_LIMIT = 56 * 1024 * 1024


def _cp(sem, vmem=None):
    return pltpu.CompilerParams(dimension_semantics=sem, vmem_limit_bytes=vmem)


def _dot(a, b):
    return jnp.dot(a, b, preferred_element_type=F32)


def _split_bf16(a):
    hi = a.astype(BF16)
    lo = (a - hi.astype(F32)).astype(BF16)
    return hi, lo


def _sigmoid(x):
    return 1.0 / (1.0 + jnp.exp(-x))


def _silu(x):
    return x * _sigmoid(x)


def _mod_kernel(c_ref, w_ref, b_ref, o_ref):
    a_hi, a_lo = _split_bf16(_silu(c_ref[...]))
    w_hi, w_lo = _split_bf16(w_ref[...])
    o_ref[...] = _dot(a_hi, w_hi) + _dot(a_lo, w_hi) + _dot(a_hi, w_lo) + b_ref[...]


def _modulation(cc, w, b):
    rows, d = cc.shape
    n = w.shape[1]
    bn = 1536
    return pl.pallas_call(
        _mod_kernel,
        grid=(n // bn,),
        in_specs=[pl.BlockSpec((rows, d), lambda j: (0, 0)),
                  pl.BlockSpec((d, bn), lambda j: (0, j)),
                  pl.BlockSpec((1, bn), lambda j: (0, j))],
        out_specs=pl.BlockSpec((rows, bn), lambda j: (0, j)),
        out_shape=jax.ShapeDtypeStruct((rows, n), F32),
        compiler_params=_cp(("arbitrary",), VMEM_LIMIT),
        name="modulation",
    )(cc, w, b.reshape(1, n))


def _rms_mod(x, nw, sh, sc):
    var = jnp.mean(x * x, axis=-1, keepdims=True)
    h = (x * lax.rsqrt(var + NORM_EPS)) * nw
    return h * (1.0 + sc) + sh


def _inproj_kernel(x_ref, nw_ref, sh_ref, sc_ref, w_ref, u_ref, r_ref, m_ref):
    h = _rms_mod(x_ref[...], nw_ref[...], sh_ref[0], sc_ref[0])
    z = _dot(h.astype(BF16), w_ref[...])
    c0 = 2 * CONV_CH
    c1 = c0 + 2 * RET_W + 2 * RET_VW
    u_ref[0] = z[:, :c0]
    r_ref[0] = z[:, c0:c1]
    m_ref[0] = z[:, c1:]


def _inproj(x_flat, row0, b, n, nw, sh, sc, w_pad):
    d = x_flat.shape[1]
    t = min(512, n)
    nt = n // t
    off = row0 // t
    wu, wr, wm = 2 * CONV_CH, 2 * RET_W + 2 * RET_VW, MLA_Q_RANK + MLA_KV_RANK + HEAD_PAD
    tok = lambda w: pl.BlockSpec((1, t, w), lambda i, j: (i, j, 0))
    per_b = pl.BlockSpec((1, 1, d), lambda i, j: (i, 0, 0))
    return pl.pallas_call(
        _inproj_kernel,
        grid=(b, nt),
        in_specs=[pl.BlockSpec((t, d), lambda i, j: (off + i * nt + j, 0)),
                  pl.BlockSpec((1, d), lambda i, j: (0, 0)), per_b, per_b,
                  pl.BlockSpec((d, IN_COLS_PAD), lambda i, j: (0, 0))],
        out_specs=[tok(wu), tok(wr), tok(wm)],
        out_shape=[jax.ShapeDtypeStruct((b, n, wu), F32),
                   jax.ShapeDtypeStruct((b, n, wr), F32),
                   jax.ShapeDtypeStruct((b, n, wm), F32)],
        compiler_params=_cp(("parallel", "parallel"), VMEM_LIMIT),
        name="norm1_inproj",
    )(x_flat, nw.reshape(1, d), sh, sc, w_pad)


_CONV_PAD = 16
_CONV_ROWS = 128


def _conv_kernel(u_ref, cw_ref, cb_ref, lw_ref, lb_ref, o_ref, hp_ref, *, n):
    c = CONV_CH
    hp_ref[0:_CONV_PAD, :] = jnp.zeros((_CONV_PAD, c), F32)
    hp_ref[n + _CONV_PAD:n + 2 * _CONV_PAD, :] = jnp.zeros((_CONV_PAD, c), F32)

    def glu(i, carry):
        r = pl.multiple_of(i * _CONV_ROWS, _CONV_ROWS)
        u = u_ref[0, pl.ds(r, _CONV_ROWS), :]
        hp_ref[pl.ds(r + _CONV_PAD, _CONV_ROWS), :] = u[:, :c] * _sigmoid(u[:, c:])
        return carry

    lax.fori_loop(0, n // _CONV_ROWS, glu, 0)

    def conv(i, carry):
        r = pl.multiple_of(i * _CONV_ROWS, _CONV_ROWS)
        base = _CONV_PAD - CONV_K // 2
        n_q = (base + CONV_K - 1) // SUBLANES + 1
        acc = jnp.zeros((_CONV_ROWS, c), F32)
        for s in range(SUBLANES):
            part = None
            for q in range(n_q):
                k = q * SUBLANES + s - base
                if 0 <= k < CONV_K:
                    win = hp_ref[pl.ds(r + q * SUBLANES, _CONV_ROWS + SUBLANES), :]
                    term = cw_ref[k:k + 1, :] * win
                    part = term if part is None else part + term
            acc = acc + part[s:s + _CONV_ROWS, :]
        hcv = acc + cb_ref[...]
        mu = jnp.mean(hcv, axis=-1, keepdims=True)
        dlt = hcv - mu
        var = jnp.mean(dlt * dlt, axis=-1, keepdims=True)
        y = (dlt * lax.rsqrt(var + NORM_EPS)) * lw_ref[...] + lb_ref[...]
        o_ref[0, pl.ds(r, _CONV_ROWS), :] = _silu(y)
        return carry

    lax.fori_loop(0, n // _CONV_ROWS, conv, 0)


def _conv(u, cw, cb, lw, lb):
    b, n, _ = u.shape
    c = CONV_CH
    vec = pl.BlockSpec((1, c), lambda i: (0, 0))
    return pl.pallas_call(
        functools.partial(_conv_kernel, n=n),
        grid=(b,),
        in_specs=[pl.BlockSpec((1, n, 2 * c), lambda i: (i, 0, 0)),
                  pl.BlockSpec((CONV_K, c), lambda i: (0, 0)), vec, vec, vec],
        out_specs=pl.BlockSpec((1, n, c), lambda i: (i, 0, 0)),
        out_shape=jax.ShapeDtypeStruct((b, n, c), F32),
        scratch_shapes=[pltpu.VMEM((n + 2 * _CONV_PAD, c), F32)],
        compiler_params=_cp(("parallel",), VMEM_LIMIT),
        name="conformer_conv",
    )(u, cw, cb.reshape(1, c), lw.reshape(1, c), lb.reshape(1, c))


def _rope_partner(x, group, lo):
    half = 16
    lane = lax.broadcasted_iota(jnp.int32, x.shape, 1) % group
    first = (lane >= lo) & (lane < lo + half)
    return jnp.where(first, pltpu.roll(x, LANES - half, 1), pltpu.roll(x, half, 1))


def _ret_kernel(lg_ref, rl_ref, rc_ref, cos_ref, sin_ref, gn_ref, avg_ref, ol_ref, oc_ref,
                q_s, k_s, yl_s, yc_s, dst_s, dq_s, dk_s, dch_s, *, n_lat, n_ctx, ctx_out):
    ch = RET_CHUNK
    lane_q = lax.broadcasted_iota(jnp.int32, (1, RET_W), 1) // RET_QK_DIM
    lane_v = lax.broadcasted_iota(jnp.int32, (1, RET_VW), 1) // RET_V_DIM
    row_h = lax.broadcasted_iota(jnp.int32, (RET_W, 1), 0) // RET_QK_DIM
    bd = (row_h == lane_v).astype(F32)
    qmask = [(lane_q == h).astype(F32) for h in range(RET_HEADS)]
    vmask = [(lane_v == h).astype(F32) for h in range(RET_HEADS)]

    ri = lax.broadcasted_iota(jnp.int32, (ch, ch), 0).astype(F32)
    ci = lax.broadcasted_iota(jnp.int32, (ch, ch), 1).astype(F32)
    rowi = lax.broadcasted_iota(jnp.int32, (ch, 1), 0).astype(F32)
    for d in range(2):
        lg_lane = jnp.zeros((1, RET_W), F32)
        lg_row = jnp.zeros((RET_W, 1), F32)
        for h in range(RET_HEADS):
            lg = lg_ref[d * RET_HEADS + h]
            lg_lane = jnp.where(lane_q == h, lg, lg_lane)
            lg_row = jnp.where(row_h == h, lg, lg_row)
            rel = (ri - ci) if d == 0 else (ci - ri)
            dst_s[d, h * ch:(h + 1) * ch, :] = jnp.where(
                rel >= 0, jnp.exp(lg * jnp.maximum(rel, 0.0)), 0.0)
        if d == 0:
            dq_s[d] = jnp.exp(lg_lane * (rowi + 1.0))
            dk_s[d] = jnp.exp(lg_lane * (ch - 1.0 - rowi))
        else:
            dq_s[d] = jnp.exp(lg_lane * (ch - rowi))
            dk_s[d] = jnp.exp(lg_lane * rowi)
        dch_s[d] = jnp.exp(lg_row * float(ch)) * jnp.ones((1, RET_VW), F32)

    kscale = RET_QK_DIM ** -0.5

    def stage(src_ref, n, rope):
        def body(i, carry):
            r = pl.multiple_of(i * ch, ch)
            q = src_ref[0, pl.ds(r, ch), 0:RET_W]
            k = src_ref[0, pl.ds(r, ch), RET_W:2 * RET_W] * kscale
            if rope:
                cs = cos_ref[pl.ds(r, ch), :]
                sn = sin_ref[pl.ds(r, ch), :]
                q = q * cs + _rope_partner(q, RET_QK_DIM, 0) * sn
                k = k * cs + _rope_partner(k, RET_QK_DIM, 0) * sn
            q_s[pl.ds(r, ch), :] = q
            k_s[pl.ds(r, ch), :] = k
            return carry
        lax.fori_loop(0, n // ch, body, 0)

    def scan(src_ref, y_ref, n, s0_fwd, s0_bwd):
        nchunks = n // ch

        def one(d, c, s):
            r = pl.multiple_of(c * ch, ch)
            qc = q_s[pl.ds(r, ch), :]
            kc = k_s[pl.ds(r, ch), :]
            vc = src_ref[0, pl.ds(r, ch), 2 * RET_W:2 * RET_W + RET_VW].astype(BF16)
            qst = jnp.concatenate([qc * qmask[h] for h in range(RET_HEADS)], axis=0).astype(BF16)
            inner = lax.dot_general(qst, kc.astype(BF16), (((1,), (1,)), ((), ())),
                                    preferred_element_type=F32) * dst_s[d]
            o = _dot(inner.astype(BF16), vc)
            y = _dot((qc * dq_s[d]).astype(BF16), s.astype(BF16))
            for h in range(RET_HEADS):
                y = y + o[h * ch:(h + 1) * ch, :] * vmask[h]
            kd_t = (kc * dk_s[d]).T.astype(BF16)
            s_new = s * dch_s[d] + _dot(kd_t, vc) * bd
            y_ref[d, pl.ds(r, ch), :] = y
            return s_new

        def body(i, carry):
            return one(0, i, carry[0]), one(1, nchunks - 1 - i, carry[1])

        return lax.fori_loop(0, nchunks, body, (s0_fwd, s0_bwd))

    def finish(src_ref, y_ref, out_ref, n):
        def body(i, carry):
            r = pl.multiple_of(i * ch, ch)
            y = y_ref[0, pl.ds(r, ch), :] + y_ref[1, pl.ds(r, ch), :]
            y_hi, y_lo = _split_bf16(y)
            mu = _dot(y_hi, avg_ref[...]) + _dot(y_lo, avg_ref[...])
            dlt = y - mu
            d_hi, d_lo = _split_bf16(dlt * dlt)
            var = _dot(d_hi, avg_ref[...]) + _dot(d_lo, avg_ref[...])
            yn = dlt * lax.rsqrt(var + NORM_EPS)
            g = src_ref[0, pl.ds(r, ch), 2 * RET_W + RET_VW:2 * RET_W + 2 * RET_VW]
            out_ref[0, pl.ds(r, ch), :] = _silu(g) * (yn * gn_ref[...])
            return carry
        lax.fori_loop(0, n // ch, body, 0)

    s_zero = jnp.zeros((RET_W, RET_VW), F32)
    stage(rc_ref, n_ctx, False)
    sc_f, sc_b = scan(rc_ref, yc_s, n_ctx, s_zero, s_zero)
    if ctx_out:
        finish(rc_ref, yc_s, oc_ref, n_ctx)
    else:
        oc_ref[...] = jnp.zeros(oc_ref.shape, F32)
    stage(rl_ref, n_lat, True)
    scan(rl_ref, yl_s, n_lat, sc_f, sc_b)
    finish(rl_ref, yl_s, ol_ref, n_lat)


def _retention(r_lat, r_ctx, log_gamma, cos_t, sin_t, gn_w, ctx_out):
    b, n_lat, w = r_lat.shape
    n_ctx = r_ctx.shape[1]
    grp = jnp.arange(RET_VW) // RET_V_DIM
    avg = ((grp[:, None] == grp[None, :]).astype(F32) / RET_V_DIM).astype(BF16)
    tab = pl.BlockSpec((n_lat, RET_W), lambda i, lg: (0, 0))
    grid_spec = pltpu.PrefetchScalarGridSpec(
        num_scalar_prefetch=1,
        grid=(b,),
        in_specs=[pl.BlockSpec((1, n_lat, w), lambda i, lg: (i, 0, 0)),
                  pl.BlockSpec((1, n_ctx, w), lambda i, lg: (i, 0, 0)),
                  tab, tab,
                  pl.BlockSpec((1, RET_VW), lambda i, lg: (0, 0)),
                  pl.BlockSpec((RET_VW, RET_VW), lambda i, lg: (0, 0))],
        out_specs=[pl.BlockSpec((1, n_lat, RET_VW), lambda i, lg: (i, 0, 0)),
                   pl.BlockSpec((1, n_ctx, RET_VW), lambda i, lg: (i, 0, 0))],
        scratch_shapes=[pltpu.VMEM((max(n_lat, n_ctx), RET_W), F32),
                        pltpu.VMEM((max(n_lat, n_ctx), RET_W), F32),
                        pltpu.VMEM((2, n_lat, RET_VW), F32),
                        pltpu.VMEM((2, n_ctx, RET_VW), F32),
                        pltpu.VMEM((2, RET_HEADS * RET_CHUNK, RET_CHUNK), F32),
                        pltpu.VMEM((2, RET_CHUNK, RET_W), F32),
                        pltpu.VMEM((2, RET_CHUNK, RET_W), F32),
                        pltpu.VMEM((2, RET_W, RET_VW), F32)],
    )
    return pl.pallas_call(
        functools.partial(_ret_kernel, n_lat=n_lat, n_ctx=n_ctx, ctx_out=ctx_out),
        grid_spec=grid_spec,
        out_shape=[jax.ShapeDtypeStruct((b, n_lat, RET_VW), F32),
                   jax.ShapeDtypeStruct((b, n_ctx, RET_VW), F32)],
        compiler_params=_cp(("parallel",), VMEM_LIMIT),
        name="retention",
    )(log_gamma.reshape(-1), r_lat, r_ctx, cos_t, sin_t, gn_w.reshape(1, RET_VW), avg)


def _mla_proj_kernel(m_ref, qn_ref, kvn_ref, wq_ref, wk_ref, wv_ref, cos_ref, sin_ref,
                     q_ref, k_ref, v_ref, *, rope):
    m = m_ref[0]
    cq = m[:, :MLA_Q_RANK]
    ckv = m[:, MLA_Q_RANK:MLA_Q_RANK + MLA_KV_RANK]
    kr = m[:, MLA_Q_RANK + MLA_KV_RANK:]

    def rms(x, w):
        var = jnp.mean(x * x, axis=-1, keepdims=True)
        return (x * lax.rsqrt(var + NORM_EPS)) * w

    q = _dot(rms(cq, qn_ref[...]).astype(BF16), wq_ref[...])
    ckv_n = rms(ckv, kvn_ref[...]).astype(BF16)
    k = _dot(ckv_n, wk_ref[...])
    v_ref[0] = _dot(ckv_n, wv_ref[...]).astype(BF16)
    if rope:
        cs = cos_ref[...]
        sn = sin_ref[...]
        kr = kr * cs + _rope_partner(kr, HEAD_PAD, MLA_NOPE_DIM) * sn
    scale = (MLA_NOPE_DIM + MLA_ROPE_DIM) ** -0.5
    for h in range(MLA_HEADS):
        sl = slice(h * HEAD_PAD, (h + 1) * HEAD_PAD)
        qh = q[:, sl]
        if rope:
            qh = qh * cs + _rope_partner(qh, HEAD_PAD, MLA_NOPE_DIM) * sn
        q_ref[0, :, sl] = (qh * scale).astype(BF16)
        k_ref[0, :, sl] = (k[:, sl] + kr).astype(BF16)


def _mla_proj(m, qn_w, kvn_w, wq_pad, wk_pad, wv, cos_t, sin_t, rope):
    b, n, w = m.shape
    t = min(512, n)
    hw = MLA_HEADS * HEAD_PAD
    vw = MLA_HEADS * MLA_V_DIM
    const = lambda shape: pl.BlockSpec(shape, lambda i, j: (0, 0))
    tab = pl.BlockSpec((t, HEAD_PAD), lambda i, j: (j, 0))
    return pl.pallas_call(
        functools.partial(_mla_proj_kernel, rope=rope),
        grid=(b, n // t),
        in_specs=[pl.BlockSpec((1, t, w), lambda i, j: (i, j, 0)),
                  const((1, MLA_Q_RANK)), const((1, MLA_KV_RANK)),
                  const((MLA_Q_RANK, hw)), const((MLA_KV_RANK, hw)), const((MLA_KV_RANK, vw)),
                  tab, tab],
        out_specs=[pl.BlockSpec((1, t, hw), lambda i, j: (i, j, 0)),
                   pl.BlockSpec((1, t, hw), lambda i, j: (i, j, 0)),
                   pl.BlockSpec((1, t, vw), lambda i, j: (i, j, 0))],
        out_shape=[jax.ShapeDtypeStruct((b, n, hw), BF16),
                   jax.ShapeDtypeStruct((b, n, hw), BF16),
                   jax.ShapeDtypeStruct((b, n, vw), BF16)],
        compiler_params=_cp(("parallel", "parallel"), VMEM_LIMIT),
        name="mla_proj",
    )(m, qn_w.reshape(1, -1), kvn_w.reshape(1, -1), wq_pad, wk_pad, wv, cos_t, sin_t)


def _attn_kernel(*refs, n_seg):
    q_ref = refs[0]
    k_refs = refs[1:1 + n_seg]
    v_refs = refs[1 + n_seg:1 + 2 * n_seg]
    o_ref = refs[1 + 2 * n_seg]
    tq = q_ref.shape[1]
    lane = lax.broadcasted_iota(jnp.int32, (tq, 2 * MLA_V_DIM), 1)
    for pair in range(MLA_HEADS // 2):
        outs = []
        for h in (2 * pair, 2 * pair + 1):
            sl = slice(h * HEAD_PAD, (h + 1) * HEAD_PAD)
            qh = q_ref[0, :, sl]
            s = [lax.dot_general(qh, kr[0, :, sl], (((1,), (1,)), ((), ())),
                                 preferred_element_type=F32) for kr in k_refs]
            mx = s[0].max(axis=-1, keepdims=True)
            for si in s[1:]:
                mx = jnp.maximum(mx, si.max(axis=-1, keepdims=True))
            den = jnp.zeros((tq, 1), F32)
            acc = jnp.zeros((tq, 2 * MLA_V_DIM), F32)
            for si, vr in zip(s, v_refs):
                p = jnp.exp(si - mx)
                den = den + p.sum(axis=-1, keepdims=True)
                acc = acc + _dot(p.astype(BF16), vr[0, :, pair * 2 * MLA_V_DIM:(pair + 1) * 2 * MLA_V_DIM])
            outs.append(acc * (1.0 / den))
        o_ref[0, :, pair * 2 * MLA_V_DIM:(pair + 1) * 2 * MLA_V_DIM] = jnp.where(
            lane < MLA_V_DIM, outs[0], outs[1])


def _attention(q, ks, vs):
    b, nq, hw = q.shape
    tq = min(512, nq)
    vw = MLA_HEADS * MLA_V_DIM
    n_seg = len(ks)
    seg_spec = lambda a: pl.BlockSpec((1,) + a.shape[1:], lambda i, j: (i, 0, 0))
    return pl.pallas_call(
        functools.partial(_attn_kernel, n_seg=n_seg),
        grid=(b, nq // tq),
        in_specs=[pl.BlockSpec((1, tq, hw), lambda i, j: (i, j, 0))]
                 + [seg_spec(a) for a in ks] + [seg_spec(a) for a in vs],
        out_specs=pl.BlockSpec((1, tq, vw), lambda i, j: (i, j, 0)),
        out_shape=jax.ShapeDtypeStruct((b, nq, vw), F32),
        compiler_params=_cp(("parallel", "parallel"), VMEM_LIMIT),
        name="mla_attention",
    )(q, *ks, *vs)


def _store_token_tiles(ref, val):
    t = val.shape[0]
    for j in range(SUBLANES):
        ref[pl.ds(j, t, stride=SUBLANES), :] = val[:, j * LANES:(j + 1) * LANES]


def _load_token_tiles(ref, t):
    return jnp.concatenate([ref[pl.ds(j, t, stride=SUBLANES), :] for j in range(SUBLANES)], axis=1)


def _outproj_kernel(cv_ref, rt_ref, ml_ref, x_ref, g1_ref, wo_ref, nw_ref, sh_ref, sc_ref,
                    rwh_ref, rwl_ref, rb_ref, *rest):
    x1_ref, h2_ref, idx_ref, wt_ref = rest[-4:]
    c0, c1 = CONV_CH, CONV_CH + RET_VW
    y = (_dot(cv_ref[0].astype(BF16), wo_ref[:c0, :])
         + _dot(rt_ref[0].astype(BF16), wo_ref[c0:c1, :])
         + _dot(ml_ref[0].astype(BF16), wo_ref[c1:, :]))
    x1 = x_ref[...] + g1_ref[0] * y
    x1_ref[...] = x1
    h2 = _rms_mod(x1, nw_ref[...], sh_ref[0], sc_ref[0])
    _store_token_tiles(h2_ref, h2)
    h_hi, h_lo = _split_bf16(h2)
    nt_dot = lambda a, bm: lax.dot_general(a, bm, (((1,), (1,)), ((), ())), preferred_element_type=F32)
    logits = nt_dot(rwh_ref[...], h_hi) + nt_dot(rwh_ref[...], h_lo) + nt_dot(rwl_ref[...], h_hi)
    scores = _sigmoid(logits)
    sel = scores + rb_ref[...]
    t = scores.shape[1]
    eio = lax.broadcasted_iota(jnp.int32, (N_EXPERTS, t), 0).astype(F32)
    slot = lax.broadcasted_iota(jnp.int32, (SUBLANES, t), 0)
    idx_out = jnp.zeros((SUBLANES, t), jnp.int32)
    wt_out = jnp.zeros((SUBLANES, t), F32)
    wsum = jnp.zeros((1, t), F32)
    for k in range(TOP_K):
        mx = jnp.max(sel, axis=0, keepdims=True)
        ik = jnp.min(jnp.where(sel == mx, eio, float(N_EXPERTS)), axis=0, keepdims=True)
        hit = eio == ik
        wk = jnp.sum(jnp.where(hit, scores, 0.0), axis=0, keepdims=True)
        sel = jnp.where(hit, -jnp.inf, sel)
        idx_out = jnp.where(slot == k, ik.astype(jnp.int32), idx_out)
        wt_out = jnp.where(slot == k, wk, wt_out)
        wsum = wsum + wk
    idx_ref[...] = idx_out
    wt_ref[...] = wt_out / wsum * ROUTED_SCALE


def _outproj(conv, ret, mla, x_flat, x_row0, g1, wo, nw, sh, sc, rw_hi, rw_lo, rb, n_total, row0, carry):
    b, n, _ = conv.shape
    d = x_flat.shape[1]
    x = x_flat
    t = min(256, n)
    nt = n // t
    off = row0 // t
    x_off = x_row0 // t
    tok = lambda w: pl.BlockSpec((1, t, w), lambda i, j: (i, j, 0))
    per_b = pl.BlockSpec((1, 1, d), lambda i, j: (i, 0, 0))
    const = lambda shape: pl.BlockSpec(shape, lambda i, j: (0, 0))
    flat = lambda rows, w: pl.BlockSpec((rows, w), lambda i, j: (off + i * nt + j, 0))
    x_spec = pl.BlockSpec((t, d), lambda i, j: (x_off + i * nt + j, 0))
    in_specs = [tok(CONV_CH), tok(RET_VW), tok(MLA_HEADS * MLA_V_DIM), x_spec, per_b,
                const((d, d)), const((1, d)), per_b, per_b,
                const((N_EXPERTS, d)), const((N_EXPERTS, d)), const((N_EXPERTS, 1))]
    operands = [conv, ret, mla, x, g1, wo, nw.reshape(1, d), sh, sc, rw_hi, rw_lo, rb.reshape(N_EXPERTS, 1)]
    choice = pl.BlockSpec((SUBLANES, t), lambda i, j: (0, off + i * nt + j))
    aliases = {}
    if carry is not None:
        aliases = {len(operands) + k: k for k in range(len(carry))}
        in_specs += [pl.BlockSpec(memory_space=pl.ANY)] * len(carry)
        operands += list(carry)
    return pl.pallas_call(
        _outproj_kernel,
        grid=(b, nt),
        in_specs=in_specs,
        out_specs=[flat(t, d), flat(t * SUBLANES, LANES), choice, choice],
        out_shape=[jax.ShapeDtypeStruct((n_total, d), F32),
                   jax.ShapeDtypeStruct((n_total * SUBLANES, LANES), F32),
                   jax.ShapeDtypeStruct((SUBLANES, n_total), jnp.int32),
                   jax.ShapeDtypeStruct((SUBLANES, n_total), F32)],
        input_output_aliases=aliases,
        compiler_params=_cp(("parallel", "parallel"), VMEM_LIMIT),
        name="outproj_norm2_router",
    )(*operands)


_MOE_ROWS = 32
_MOE_RMW = 16


def _moe_kernel(be_ref, src_ref, cnt_ref, first_ref, nxt_ref, used_ref, tok_ref, wt_ref,
                h_ref, wg_hbm, wu_hbm, wd_hbm, *rest, blocks_per_tile, dump_row, layer):
    o_ref, xt_ref, yt_ref, wg_s, wu_s, wd_s, sem = rest[-7:]
    tile = pl.program_id(0)
    b0 = tile * blocks_per_tile

    def weight_copies(e, s):
        return (pltpu.make_async_copy(wg_hbm.at[layer, e], wg_s.at[s], sem.at[s, 0]),
                pltpu.make_async_copy(wu_hbm.at[layer, e], wu_s.at[s], sem.at[s, 1]),
                pltpu.make_async_copy(wd_hbm.at[layer, e], wd_s.at[s], sem.at[s, 2]))

    @pl.when(tile == 0)
    def _():
        xt_ref[...] = jnp.zeros(xt_ref.shape, F32)

    o_ref[...] = jnp.zeros(o_ref.shape, F32)
    n_used = used_ref[tile]

    @pl.when(n_used > 0)
    def _():
        for cp in weight_copies(be_ref[b0], 0):
            cp.start()

    def block(i, carry):
        b = b0 + i
        s = first_ref[b] >> 1

        @pl.when((first_ref[b] & 1) == 1)
        def _():
            for cp in weight_copies(be_ref[b], s):
                cp.wait()

            @pl.when(nxt_ref[b] >= 0)
            def _():
                for cp in weight_copies(nxt_ref[b], 1 - s):
                    cp.start()

        cnt = cnt_ref[b]
        base = src_ref[b]
        last = cnt - 1
        def gather_rows(g0, partial):
            for mi in range(g0, g0 + _MOE_ROWS):
                i = base + (jnp.minimum(mi, last) if partial else mi)
                t8 = pl.multiple_of(tok_ref[i], SUBLANES)
                xt_ref[mi * SUBLANES:(mi + 1) * SUBLANES, :] = h_ref[0, pl.ds(t8, SUBLANES), :]

        for g0 in range(0, MOE_BLOCK, _MOE_ROWS):
            pl.when(g0 + _MOE_ROWS <= cnt)(functools.partial(gather_rows, g0, False))
            pl.when((g0 < cnt) & (g0 + _MOE_ROWS > cnt))(functools.partial(gather_rows, g0, True))
        x = jnp.concatenate([xt_ref[pl.ds(j, MOE_BLOCK, stride=SUBLANES), :] for j in range(SUBLANES)],
                            axis=1).astype(BF16)
        g = _dot(x, wg_s[s])
        u = _dot(x, wu_s[s])
        y = _dot((_silu(g) * u).astype(BF16), wd_s[s])
        for j in range(SUBLANES):
            yt_ref[pl.ds(j * MOE_STRIDE, MOE_BLOCK), :] = y[:, j * LANES:(j + 1) * LANES]
        def scatter_rows(g0, partial):
            for m0 in range(g0, g0 + _MOE_ROWS, _MOE_RMW):
                rows = []
                for mi in range(m0, m0 + _MOE_RMW):
                    if partial:
                        i = base + jnp.minimum(mi, last)
                        t8 = pl.multiple_of(jnp.where(mi < cnt, tok_ref[i], dump_row), SUBLANES)
                    else:
                        i = base + mi
                        t8 = pl.multiple_of(tok_ref[i], SUBLANES)
                    upd = yt_ref[pl.ds(mi, SUBLANES, stride=MOE_STRIDE), :] * wt_ref[i]
                    rows.append((t8, o_ref[0, pl.ds(t8, SUBLANES), :] + upd))
                for t8, val in rows:
                    o_ref[0, pl.ds(t8, SUBLANES), :] = val

        for g0 in range(0, MOE_BLOCK, _MOE_ROWS):
            pl.when(g0 + _MOE_ROWS <= cnt)(functools.partial(scatter_rows, g0, False))
            pl.when((g0 < cnt) & (g0 + _MOE_ROWS > cnt))(functools.partial(scatter_rows, g0, True))
        return carry

    lax.fori_loop(0, n_used, block, 0)


def _moe_tile_size(n_tok):
    for s in (4096, 2048, 1024, 512, 256):
        if n_tok % s == 0:
            return s
    raise ValueError(f"token count {n_tok} must be a multiple of 256")


_MOE_SMEM_WORDS = 64 * 1024


def _routed_experts(h3, idx, wts, wg, wu, wd, layer, ts):
    n_tiles = h3.shape[0]
    d = SUBLANES * LANES
    n_assign = ts * TOP_K
    bpt = n_assign // MOE_BLOCK + N_EXPERTS

    e_t = idx.reshape(n_tiles, n_assign).astype(jnp.int32)
    tok8 = (jnp.arange(n_assign, dtype=jnp.int32) // TOP_K) * SUBLANES
    tok_bits = (ts * SUBLANES - 1).bit_length()
    s_key, s_w = lax.sort((e_t * (1 << tok_bits) + tok8, wts.reshape(n_tiles, n_assign)), dimension=1,
                          num_keys=1, is_stable=False)
    s_tok = s_key & ((1 << tok_bits) - 1)
    counts = jnp.sum((e_t[:, :, None] == jnp.arange(N_EXPERTS, dtype=jnp.int32)).astype(jnp.int32), axis=1)
    start = jnp.cumsum(counts, axis=1) - counts
    nblk = (counts + MOE_BLOCK - 1) // MOE_BLOCK
    blk_end = jnp.cumsum(nblk, axis=1)
    blk_start = blk_end - nblk
    used = blk_end[:, -1:]
    bi = jnp.broadcast_to(jnp.arange(bpt, dtype=jnp.int32), (n_tiles, bpt))
    bi_c = jnp.minimum(bi, used - 1)
    e_b = jnp.sum((blk_end[:, None, :] <= bi_c[:, :, None]).astype(jnp.int32), axis=2)
    take = lambda a: jnp.take_along_axis(a, e_b, axis=1)
    j = bi_c - take(blk_start)
    src = take(start) + j * MOE_BLOCK
    cnt = jnp.clip(take(counts) - j * MOE_BLOCK, 0, MOE_BLOCK)
    e_ids = jnp.arange(N_EXPERTS, dtype=jnp.int32)
    has = nblk > 0
    ordinal = jnp.cumsum(has.astype(jnp.int32), axis=1) - 1
    later = jnp.where(has, e_ids, N_EXPERTS)
    later = jnp.concatenate([later[:, 1:], jnp.full((n_tiles, 1), N_EXPERTS, jnp.int32)], axis=1)
    nxt_e = lax.cummin(later, axis=1, reverse=True)
    nxt_e = jnp.where(nxt_e == N_EXPERTS, -1, nxt_e)
    first = (take(ordinal) % 2) * 2 + (j == 0).astype(jnp.int32)
    nxt = take(nxt_e)

    n_calls = -(-n_tiles // (_MOE_SMEM_WORDS // n_assign))
    bounds = [n_tiles * c // n_calls for c in range(n_calls + 1)]
    out = None
    n_pref = 8
    for c in range(n_calls):
        t0, t1 = bounds[c], bounds[c + 1]
        group = t1 - t0
        src_abs = src[t0:t1] + jnp.arange(group, dtype=jnp.int32)[:, None] * n_assign
        tile_map = lambda i, *_, t0=t0: (t0 + i, 0, 0)
        hbm = pl.BlockSpec(memory_space=pl.ANY)
        in_specs = [pl.BlockSpec((1, ts * SUBLANES, LANES), tile_map, pipeline_mode=pl.Buffered(1)),
                    hbm, hbm, hbm]
        operands = [e_b[t0:t1].reshape(-1), src_abs.reshape(-1), cnt[t0:t1].reshape(-1),
                    first[t0:t1].reshape(-1), nxt[t0:t1].reshape(-1), used[t0:t1].reshape(-1),
                    s_tok[t0:t1].reshape(-1), s_w[t0:t1].reshape(-1), h3, wg, wu, wd]
        aliases = {}
        if out is not None:
            in_specs.append(hbm)
            aliases = {len(operands): 0}
            operands.append(out)
        grid_spec = pltpu.PrefetchScalarGridSpec(
            num_scalar_prefetch=n_pref,
            grid=(group,),
            in_specs=in_specs,
            out_specs=pl.BlockSpec((1, (ts + 1) * SUBLANES, LANES), tile_map, pipeline_mode=pl.Buffered(1)),
            scratch_shapes=[pltpu.VMEM((MOE_BLOCK * SUBLANES, LANES), F32),
                            pltpu.VMEM((SUBLANES * MOE_STRIDE, LANES), F32),
                            pltpu.VMEM((2, d, EXPERT_DIM), BF16),
                            pltpu.VMEM((2, d, EXPERT_DIM), BF16),
                            pltpu.VMEM((2, EXPERT_DIM, d), BF16),
                            pltpu.SemaphoreType.DMA((2, 3))],
        )
        out = pl.pallas_call(
            functools.partial(_moe_kernel, blocks_per_tile=bpt, dump_row=ts * SUBLANES, layer=layer),
            grid_spec=grid_spec,
            out_shape=jax.ShapeDtypeStruct((n_tiles, (ts + 1) * SUBLANES, LANES), F32),
            input_output_aliases=aliases,
            compiler_params=_cp(("arbitrary",), VMEM_LIMIT),
            name="routed_experts",
        )(*operands)
    return out


def _ffn_out_kernel(x_ref, h_ref, r_ref, g2_ref, sg_ref, su_ref, sd_ref, fw_ref, o_ref, *, final):
    t = x_ref.shape[0]
    h = _load_token_tiles(h_ref.at[0], t).astype(BF16)
    a = _silu(_dot(h, sg_ref[...])) * _dot(h, su_ref[...])
    y = _load_token_tiles(r_ref.at[0], t) + _dot(a.astype(BF16), sd_ref[...])
    x2 = x_ref[...] + g2_ref[0] * y
    if final:
        var = jnp.mean(x2 * x2, axis=-1, keepdims=True)
        x2 = (x2 * lax.rsqrt(var + NORM_EPS)) * fw_ref[...]
    o_ref[...] = x2


def _ffn_out(x1, h3, routed, ts, g2_rows, rows_per_gate, sg, su, sd, fw, final):
    n_tok, d = x1.shape
    t = 512 if (rows_per_gate % 512 == 0 and n_tok % 512 == 0 and ts % 512 == 0) else 256
    tiles_per_row = rows_per_gate // t
    per = ts // t
    last = g2_rows.shape[0] - 1
    tok = pl.BlockSpec((t, d), lambda i: (i, 0))
    tiles = pl.BlockSpec((1, t * SUBLANES, LANES), lambda i: (i // per, i % per, 0))
    const = lambda shape: pl.BlockSpec(shape, lambda i: (0, 0))
    return pl.pallas_call(
        functools.partial(_ffn_out_kernel, final=final),
        grid=(n_tok // t,),
        in_specs=[tok, tiles, tiles,
                  pl.BlockSpec((1, 1, d), lambda i: (jnp.minimum(i // tiles_per_row, last), 0, 0)),
                  const((d, EXPERT_DIM)), const((d, EXPERT_DIM)), const((EXPERT_DIM, d)), const((1, d))],
        out_specs=tok,
        out_shape=jax.ShapeDtypeStruct((n_tok, d), F32),
        compiler_params=_cp(("parallel",), VMEM_LIMIT),
        name="shared_expert_residual",
    )(x1, h3, routed, g2_rows, sg, su, sd, fw.reshape(1, d))


def _rope_tables(rows, dim, group, lo):
    pos_r = jnp.repeat(jnp.arange(rows, dtype=F32), GRID_W)
    pos_c = jnp.tile(jnp.arange(GRID_W, dtype=F32), rows)
    n_freq = dim // 4
    inv = ROPE_BASE ** (-jnp.arange(n_freq, dtype=F32) / n_freq)
    ang = jnp.concatenate([pos_r[:, None] * inv, pos_c[:, None] * inv], axis=-1)
    cos, sin = jnp.cos(ang), jnp.sin(ang)
    n = rows * GRID_W
    half = dim // 2
    cos_g = jnp.ones((n, group), F32).at[:, lo:lo + dim].set(jnp.concatenate([cos, cos], axis=-1))
    sin_g = jnp.zeros((n, group), F32).at[:, lo:lo + dim].set(jnp.concatenate([-sin, sin], axis=-1))
    reps = LANES // group
    return jnp.tile(cos_g, (1, reps)), jnp.tile(sin_g, (1, reps))


def _pad_in_proj(w_in):
    d = w_in.shape[0]
    body = w_in[:, :IN_COLS_PAD - HEAD_PAD]
    kr = w_in[:, IN_COLS_PAD - HEAD_PAD:]
    kr_pad = jnp.zeros((d, HEAD_PAD), w_in.dtype).at[:, MLA_NOPE_DIM:MLA_NOPE_DIM + MLA_ROPE_DIM].set(kr)
    return jnp.concatenate([body, kr_pad], axis=1).astype(BF16)


def _pad_heads(w, width):
    k = w.shape[0]
    w3 = w.reshape(k, MLA_HEADS, width)
    return jnp.zeros((k, MLA_HEADS, HEAD_PAD), w.dtype).at[:, :, :width].set(w3).reshape(
        k, MLA_HEADS * HEAD_PAD).astype(BF16)


def kernel(x, c, ctx, c_ctx, mod_w, mod_b, norm1_w, w_in, conv_w, conv_b, conv_ln_w, conv_ln_b,
           ret_decay_logit, ret_gn_w, q_norm_w, w_uq, kv_norm_w, w_ukv, w_out, norm2_w,
           router_w, router_b, exp_w_gate, exp_w_up, exp_w_down, sh_w_gate, sh_w_up, sh_w_down,
           final_norm_w):
    b, n_lat, d = x.shape
    n_ctx = ctx.shape[1]
    depth = mod_w.shape[0]
    rows = n_lat // GRID_W
    cos_ret, sin_ret = _rope_tables(rows, RET_QK_DIM, RET_QK_DIM, 0)
    cos_mla, sin_mla = _rope_tables(rows, MLA_ROPE_DIM, HEAD_PAD, MLA_NOPE_DIM)

    mod_rows = -(-(b + 1) // SUBLANES) * SUBLANES
    cc = jnp.zeros((mod_rows, d), F32).at[:b].set(c).at[b].set(c_ctx)

    n_l = b * n_lat
    wg_all, wu_all, wd_all = exp_w_gate.astype(BF16), exp_w_up.astype(BF16), exp_w_down.astype(BF16)
    xl, xl_row = x.reshape(n_l, d), 0
    xc, xc_row = ctx.reshape(b * n_ctx, d), 0
    for i in range(depth):
        last = i == depth - 1
        mod = _modulation(cc, mod_w[i], mod_b[i])
        ml = mod[:b].reshape(b, 1, 6, d)
        sh1, sc1, g1, sh2, sc2, g2 = [ml[:, :, j, :] for j in range(6)]
        mc = jnp.broadcast_to(mod[b].reshape(1, 1, 6, d), (b, 1, 6, d))
        csh1, csc1, cg1, csh2, csc2, cg2 = [mc[:, :, j, :] for j in range(6)]

        w_in_p = _pad_in_proj(w_in[i])
        ul, rl, mlat = _inproj(xl, xl_row, b, n_lat, norm1_w[i], sh1, sc1, w_in_p)
        uc, rc, mctx = _inproj(xc, xc_row, b, n_ctx, norm1_w[i], csh1, csc1, w_in_p)

        conv_l = _conv(ul, conv_w[i], conv_b[i], conv_ln_w[i], conv_ln_b[i])
        log_gamma = jax.nn.log_sigmoid(ret_decay_logit[i].astype(F32))
        ret_l, ret_c = _retention(rl, rc, log_gamma, cos_ret, sin_ret, ret_gn_w[i], not last)

        wq_p = _pad_heads(w_uq[i], MLA_NOPE_DIM + MLA_ROPE_DIM)
        wkv = w_ukv[i].reshape(MLA_KV_RANK, MLA_HEADS, MLA_NOPE_DIM + MLA_V_DIM)
        wk_p = _pad_heads(wkv[:, :, :MLA_NOPE_DIM].reshape(MLA_KV_RANK, -1), MLA_NOPE_DIM)
        wv = wkv[:, :, MLA_NOPE_DIM:].reshape(MLA_KV_RANK, -1).astype(BF16)
        ql, kl, vl = _mla_proj(mlat, q_norm_w[i], kv_norm_w[i], wq_p, wk_p, wv, cos_mla, sin_mla, True)
        qc, kc, vc = _mla_proj(mctx, q_norm_w[i], kv_norm_w[i], wq_p, wk_p, wv,
                               cos_mla[:n_ctx], sin_mla[:n_ctx], False)
        mla_l = _attention(ql, [kc, kl], [vc, vl])

        wo = w_out[i].astype(BF16)
        rw_t = router_w[i].T
        rw_hi = rw_t.astype(BF16)
        rw_lo = (rw_t - rw_hi.astype(F32)).astype(BF16)
        n_tok = n_l if last else n_l + b * n_ctx
        outs = _outproj(conv_l, ret_l, mla_l, xl, xl_row, g1, wo, norm2_w[i], sh2, sc2,
                        rw_hi, rw_lo, router_b[i], n_tok, 0, None)
        g2_rows = g2
        if not last:
            conv_c = _conv(uc, conv_w[i], conv_b[i], conv_ln_w[i], conv_ln_b[i])
            mla_c = _attention(qc, [kc], [vc])
            outs = _outproj(conv_c, ret_c, mla_c, xc, xc_row, cg1, wo, norm2_w[i], csh2, csc2,
                            rw_hi, rw_lo, router_b[i], n_tok, n_l, outs)
            g2_rows = jnp.concatenate([g2, cg2[:1]], axis=0)
        x1, h2t, idx, wts = outs
        idx = idx[:TOP_K].T
        wts = wts[:TOP_K].T

        ts = _moe_tile_size(n_tok)
        h3 = h2t.reshape(n_tok // ts, ts * SUBLANES, LANES)
        routed = _routed_experts(h3, idx, wts, wg_all, wu_all, wd_all, i, ts)
        x2 = _ffn_out(x1, h3, routed, ts, g2_rows, n_lat, sh_w_gate[i].astype(BF16),
                      sh_w_up[i].astype(BF16), sh_w_down[i].astype(BF16), final_norm_w, last)
        xl, xl_row = x2, 0
        xc, xc_row = x2, n_l
    return xl.reshape(b, n_lat, d)
```

```python
import functools

import jax
import jax.numpy as jnp
from jax import lax
from jax.experimental import pallas as pl
from jax.experimental.pallas import tpu as pltpu

F32 = jnp.float32
BF16 = jnp.bfloat16

D_MODEL = 1024
GRID_W = 64
CONV_CH = 256
CONV_K = 31
RET_HEADS = 4
RET_QK_DIM = 32
RET_V_DIM = 64
RET_CHUNK = 128
MLA_HEADS = 8
MLA_NOPE_DIM = 64
MLA_ROPE_DIM = 32
MLA_V_DIM = 64
MLA_Q_RANK = 256
MLA_KV_RANK = 128
ROPE_BASE = 10000.0
N_EXPERTS = 64
TOP_K = 6
EXPERT_DIM = 256
ROUTED_SCALE = 2.5
NORM_EPS = 1e-6
_LOG2_E = 1.4426950408889634

LANES = 128
SUBLANES = 8
HEAD_PAD = 128
RET_W = RET_HEADS * RET_QK_DIM
RET_VW = RET_HEADS * RET_V_DIM
IN_COLS_PAD = 2 * CONV_CH + 2 * RET_W + 2 * RET_VW + MLA_Q_RANK + MLA_KV_RANK + HEAD_PAD
MOE_BLOCK = 256
MOE_STRIDE = MOE_BLOCK + SUBLANES
VMEM_LIMIT = 56 * 1024 * 1024


def _cp(sem, vmem=None):
    return pltpu.CompilerParams(dimension_semantics=sem, vmem_limit_bytes=vmem)


def _dot(a, b):
    return jnp.dot(a, b, preferred_element_type=F32)


def _split_bf16(a):
    hi = a.astype(BF16)
    lo = (a - hi.astype(F32)).astype(BF16)
    return hi, lo


def _sigmoid(x):
    return 1.0 / (1.0 + jnp.exp(-x))


def _silu(x):
    return x * _sigmoid(x)


def _mod_kernel(c_ref, w_ref, b_ref, o_ref):
    a_hi, a_lo = _split_bf16(_silu(c_ref[...]))
    w_hi, w_lo = _split_bf16(w_ref[...])
    o_ref[...] = _dot(a_hi, w_hi) + _dot(a_lo, w_hi) + _dot(a_hi, w_lo) + b_ref[...]


def _modulation(cc, w, b):
    rows, d = cc.shape
    n = w.shape[1]
    bn = 1536
    return pl.pallas_call(
        _mod_kernel,
        grid=(n // bn,),
        in_specs=[pl.BlockSpec((rows, d), lambda j: (0, 0)),
                  pl.BlockSpec((d, bn), lambda j: (0, j)),
                  pl.BlockSpec((1, bn), lambda j: (0, j))],
        out_specs=pl.BlockSpec((rows, bn), lambda j: (0, j)),
        out_shape=jax.ShapeDtypeStruct((rows, n), F32),
        compiler_params=_cp(("arbitrary",), VMEM_LIMIT),
        name="modulation",
    )(cc, w, b.reshape(1, n))


def _rms_mod(x, nw, sh, sc):
    var = jnp.mean(x * x, axis=-1, keepdims=True)
    h = (x * lax.rsqrt(var + NORM_EPS)) * nw
    return h * (1.0 + sc) + sh


def _inproj_kernel(x_ref, nw_ref, sh_ref, sc_ref, w_ref, u_ref, r_ref, m_ref):
    h = _rms_mod(x_ref[...], nw_ref[...], sh_ref[0], sc_ref[0])
    z = _dot(h.astype(BF16), w_ref[...])
    c0 = 2 * CONV_CH
    c1 = c0 + 2 * RET_W + 2 * RET_VW
    u_ref[0] = z[:, :c0]
    r_ref[0] = z[:, c0:c1]
    m_ref[0] = z[:, c1:]


def _inproj(x_flat, row0, b, n, nw, sh, sc, w_pad):
    d = x_flat.shape[1]
    t = min(512, n)
    nt = n // t
    off = row0 // t
    wu, wr, wm = 2 * CONV_CH, 2 * RET_W + 2 * RET_VW, MLA_Q_RANK + MLA_KV_RANK + HEAD_PAD
    tok = lambda w: pl.BlockSpec((1, t, w), lambda i, j: (i, j, 0))
    per_b = pl.BlockSpec((1, 1, d), lambda i, j: (i, 0, 0))
    return pl.pallas_call(
        _inproj_kernel,
        grid=(b, nt),
        in_specs=[pl.BlockSpec((t, d), lambda i, j: (off + i * nt + j, 0)),
                  pl.BlockSpec((1, d), lambda i, j: (0, 0)), per_b, per_b,
                  pl.BlockSpec((d, IN_COLS_PAD), lambda i, j: (0, 0))],
        out_specs=[tok(wu), tok(wr), tok(wm)],
        out_shape=[jax.ShapeDtypeStruct((b, n, wu), F32),
                   jax.ShapeDtypeStruct((b, n, wr), F32),
                   jax.ShapeDtypeStruct((b, n, wm), F32)],
        compiler_params=_cp(("parallel", "parallel"), VMEM_LIMIT),
        name="norm1_inproj",
    )(x_flat, nw.reshape(1, d), sh, sc, w_pad)


_CONV_PAD = 16
_CONV_ROWS = 128


def _conv_kernel(u_ref, cw_ref, cb_ref, lw_ref, lb_ref, o_ref, hp_ref, *, n):
    c = CONV_CH
    hp_ref[0:_CONV_PAD, :] = jnp.zeros((_CONV_PAD, c), F32)
    hp_ref[n + _CONV_PAD:n + 2 * _CONV_PAD, :] = jnp.zeros((_CONV_PAD, c), F32)

    def glu(i, carry):
        r = pl.multiple_of(i * _CONV_ROWS, _CONV_ROWS)
        u = u_ref[0, pl.ds(r, _CONV_ROWS), :]
        hp_ref[pl.ds(r + _CONV_PAD, _CONV_ROWS), :] = u[:, :c] * _sigmoid(u[:, c:])
        return carry

    lax.fori_loop(0, n // _CONV_ROWS, glu, 0)

    def conv(i, carry):
        r = pl.multiple_of(i * _CONV_ROWS, _CONV_ROWS)
        base = _CONV_PAD - CONV_K // 2
        n_q = (base + CONV_K - 1) // SUBLANES + 1
        acc = jnp.zeros((_CONV_ROWS, c), F32)
        for s in range(SUBLANES):
            part = None
            for q in range(n_q):
                k = q * SUBLANES + s - base
                if 0 <= k < CONV_K:
                    win = hp_ref[pl.ds(r + q * SUBLANES, _CONV_ROWS + SUBLANES), :]
                    term = cw_ref[k:k + 1, :] * win
                    part = term if part is None else part + term
            acc = acc + part[s:s + _CONV_ROWS, :]
        hcv = acc + cb_ref[...]
        mu = jnp.mean(hcv, axis=-1, keepdims=True)
        dlt = hcv - mu
        var = jnp.mean(dlt * dlt, axis=-1, keepdims=True)
        y = (dlt * lax.rsqrt(var + NORM_EPS)) * lw_ref[...] + lb_ref[...]
        o_ref[0, pl.ds(r, _CONV_ROWS), :] = _silu(y)
        return carry

    lax.fori_loop(0, n // _CONV_ROWS, conv, 0)


def _conv(u, cw, cb, lw, lb):
    b, n, _ = u.shape
    c = CONV_CH
    vec = pl.BlockSpec((1, c), lambda i: (0, 0))
    return pl.pallas_call(
        functools.partial(_conv_kernel, n=n),
        grid=(b,),
        in_specs=[pl.BlockSpec((1, n, 2 * c), lambda i: (i, 0, 0)),
                  pl.BlockSpec((CONV_K, c), lambda i: (0, 0)), vec, vec, vec],
        out_specs=pl.BlockSpec((1, n, c), lambda i: (i, 0, 0)),
        out_shape=jax.ShapeDtypeStruct((b, n, c), F32),
        scratch_shapes=[pltpu.VMEM((n + 2 * _CONV_PAD, c), F32)],
        compiler_params=_cp(("parallel",), VMEM_LIMIT),
        name="conformer_conv",
    )(u, cw, cb.reshape(1, c), lw.reshape(1, c), lb.reshape(1, c))


def _rope_partner(x, group, lo):
    half = 16
    lane = lax.broadcasted_iota(jnp.int32, x.shape, 1) % group
    first = (lane >= lo) & (lane < lo + half)
    return jnp.where(first, pltpu.roll(x, LANES - half, 1), pltpu.roll(x, half, 1))


def _ret_kernel(lg_ref, rl_ref, rc_ref, cos_ref, sin_ref, gn_ref, avg_ref, ol_ref, oc_ref,
                q_s, k_s, yl_s, yc_s, dst_s, dq_s, dk_s, dch_s, *, n_lat, n_ctx, ctx_out):
    ch = RET_CHUNK
    lane_q = lax.broadcasted_iota(jnp.int32, (1, RET_W), 1) // RET_QK_DIM
    lane_v = lax.broadcasted_iota(jnp.int32, (1, RET_VW), 1) // RET_V_DIM
    row_h = lax.broadcasted_iota(jnp.int32, (RET_W, 1), 0) // RET_QK_DIM
    bd = (row_h == lane_v).astype(F32)
    qmask = [(lane_q == h).astype(F32) for h in range(RET_HEADS)]
    vmask = [(lane_v == h).astype(F32) for h in range(RET_HEADS)]

    ri = lax.broadcasted_iota(jnp.int32, (ch, ch), 0).astype(F32)
    ci = lax.broadcasted_iota(jnp.int32, (ch, ch), 1).astype(F32)
    rowi = lax.broadcasted_iota(jnp.int32, (ch, 1), 0).astype(F32)
    for d in range(2):
        lg_lane = jnp.zeros((1, RET_W), F32)
        lg_row = jnp.zeros((RET_W, 1), F32)
        for h in range(RET_HEADS):
            lg = lg_ref[d * RET_HEADS + h]
            lg_lane = jnp.where(lane_q == h, lg, lg_lane)
            lg_row = jnp.where(row_h == h, lg, lg_row)
            rel = (ri - ci) if d == 0 else (ci - ri)
            dst_s[d, h * ch:(h + 1) * ch, :] = jnp.where(
                rel >= 0, jnp.exp(lg * jnp.maximum(rel, 0.0)), 0.0)
        if d == 0:
            dq_s[d] = jnp.exp(lg_lane * (rowi + 1.0))
            dk_s[d] = jnp.exp(lg_lane * (ch - 1.0 - rowi))
        else:
            dq_s[d] = jnp.exp(lg_lane * (ch - rowi))
            dk_s[d] = jnp.exp(lg_lane * rowi)
        dch_s[d] = jnp.exp(lg_row * float(ch)) * jnp.ones((1, RET_VW), F32)

    kscale = RET_QK_DIM ** -0.5

    def stage(src_ref, n, rope):
        def body(i, carry):
            r = pl.multiple_of(i * ch, ch)
            q = src_ref[0, pl.ds(r, ch), 0:RET_W]
            k = src_ref[0, pl.ds(r, ch), RET_W:2 * RET_W] * kscale
            if rope:
                cs = cos_ref[pl.ds(r, ch), :]
                sn = sin_ref[pl.ds(r, ch), :]
                q = q * cs + _rope_partner(q, RET_QK_DIM, 0) * sn
                k = k * cs + _rope_partner(k, RET_QK_DIM, 0) * sn
            q_s[pl.ds(r, ch), :] = q
            k_s[pl.ds(r, ch), :] = k
            return carry
        lax.fori_loop(0, n // ch, body, 0)

    def scan(src_ref, y_ref, n, s0_fwd, s0_bwd):
        nchunks = n // ch

        def one(d, c, s):
            r = pl.multiple_of(c * ch, ch)
            qc = q_s[pl.ds(r, ch), :]
            kc = k_s[pl.ds(r, ch), :]
            vc = src_ref[0, pl.ds(r, ch), 2 * RET_W:2 * RET_W + RET_VW].astype(BF16)
            qst = jnp.concatenate([qc * qmask[h] for h in range(RET_HEADS)], axis=0).astype(BF16)
            inner = lax.dot_general(qst, kc.astype(BF16), (((1,), (1,)), ((), ())),
                                    preferred_element_type=F32) * dst_s[d]
            o = _dot(inner.astype(BF16), vc)
            y = _dot((qc * dq_s[d]).astype(BF16), s.astype(BF16))
            for h in range(RET_HEADS):
                y = y + o[h * ch:(h + 1) * ch, :] * vmask[h]
            kd_t = (kc * dk_s[d]).T.astype(BF16)
            s_new = s * dch_s[d] + _dot(kd_t, vc) * bd
            y_ref[d, pl.ds(r, ch), :] = y
            return s_new

        def body(i, carry):
            return one(0, i, carry[0]), one(1, nchunks - 1 - i, carry[1])

        return lax.fori_loop(0, nchunks, body, (s0_fwd, s0_bwd))

    def finish(src_ref, y_ref, out_ref, n):
        def body(i, carry):
            r = pl.multiple_of(i * ch, ch)
            y = y_ref[0, pl.ds(r, ch), :] + y_ref[1, pl.ds(r, ch), :]
            y_hi, y_lo = _split_bf16(y)
            mu = _dot(y_hi, avg_ref[...]) + _dot(y_lo, avg_ref[...])
            dlt = y - mu
            d_hi, d_lo = _split_bf16(dlt * dlt)
            var = _dot(d_hi, avg_ref[...]) + _dot(d_lo, avg_ref[...])
            yn = dlt * lax.rsqrt(var + NORM_EPS)
            g = src_ref[0, pl.ds(r, ch), 2 * RET_W + RET_VW:2 * RET_W + 2 * RET_VW]
            out_ref[0, pl.ds(r, ch), :] = _silu(g) * (yn * gn_ref[...])
            return carry
        lax.fori_loop(0, n // ch, body, 0)

    s_zero = jnp.zeros((RET_W, RET_VW), F32)
    stage(rc_ref, n_ctx, False)
    sc_f, sc_b = scan(rc_ref, yc_s, n_ctx, s_zero, s_zero)
    if ctx_out:
        finish(rc_ref, yc_s, oc_ref, n_ctx)
    else:
        oc_ref[...] = jnp.zeros(oc_ref.shape, F32)
    stage(rl_ref, n_lat, True)
    scan(rl_ref, yl_s, n_lat, sc_f, sc_b)
    finish(rl_ref, yl_s, ol_ref, n_lat)


def _retention(r_lat, r_ctx, log_gamma, cos_t, sin_t, gn_w, ctx_out):
    b, n_lat, w = r_lat.shape
    n_ctx = r_ctx.shape[1]
    grp = jnp.arange(RET_VW) // RET_V_DIM
    avg = ((grp[:, None] == grp[None, :]).astype(F32) / RET_V_DIM).astype(BF16)
    tab = pl.BlockSpec((n_lat, RET_W), lambda i, lg: (0, 0))
    grid_spec = pltpu.PrefetchScalarGridSpec(
        num_scalar_prefetch=1,
        grid=(b,),
        in_specs=[pl.BlockSpec((1, n_lat, w), lambda i, lg: (i, 0, 0)),
                  pl.BlockSpec((1, n_ctx, w), lambda i, lg: (i, 0, 0)),
                  tab, tab,
                  pl.BlockSpec((1, RET_VW), lambda i, lg: (0, 0)),
                  pl.BlockSpec((RET_VW, RET_VW), lambda i, lg: (0, 0))],
        out_specs=[pl.BlockSpec((1, n_lat, RET_VW), lambda i, lg: (i, 0, 0)),
                   pl.BlockSpec((1, n_ctx, RET_VW), lambda i, lg: (i, 0, 0))],
        scratch_shapes=[pltpu.VMEM((max(n_lat, n_ctx), RET_W), F32),
                        pltpu.VMEM((max(n_lat, n_ctx), RET_W), F32),
                        pltpu.VMEM((2, n_lat, RET_VW), F32),
                        pltpu.VMEM((2, n_ctx, RET_VW), F32),
                        pltpu.VMEM((2, RET_HEADS * RET_CHUNK, RET_CHUNK), F32),
                        pltpu.VMEM((2, RET_CHUNK, RET_W), F32),
                        pltpu.VMEM((2, RET_CHUNK, RET_W), F32),
                        pltpu.VMEM((2, RET_W, RET_VW), F32)],
    )
    return pl.pallas_call(
        functools.partial(_ret_kernel, n_lat=n_lat, n_ctx=n_ctx, ctx_out=ctx_out),
        grid_spec=grid_spec,
        out_shape=[jax.ShapeDtypeStruct((b, n_lat, RET_VW), F32),
                   jax.ShapeDtypeStruct((b, n_ctx, RET_VW), F32)],
        compiler_params=_cp(("parallel",), VMEM_LIMIT),
        name="retention",
    )(log_gamma.reshape(-1), r_lat, r_ctx, cos_t, sin_t, gn_w.reshape(1, RET_VW), avg)


def _mla_proj_kernel(m_ref, qn_ref, kvn_ref, wq_ref, wk_ref, wv_ref, cos_ref, sin_ref,
                     q_ref, k_ref, v_ref, *, rope):
    m = m_ref[0]
    cq = m[:, :MLA_Q_RANK]
    ckv = m[:, MLA_Q_RANK:MLA_Q_RANK + MLA_KV_RANK]
    kr = m[:, MLA_Q_RANK + MLA_KV_RANK:]

    def rms(x, w):
        var = jnp.mean(x * x, axis=-1, keepdims=True)
        return (x * lax.rsqrt(var + NORM_EPS)) * w

    q = _dot(rms(cq, qn_ref[...]).astype(BF16), wq_ref[...])
    ckv_n = rms(ckv, kvn_ref[...]).astype(BF16)
    k = _dot(ckv_n, wk_ref[...])
    v_ref[0] = _dot(ckv_n, wv_ref[...]).astype(BF16)
    if rope:
        cs = cos_ref[...]
        sn = sin_ref[...]
        kr = kr * cs + _rope_partner(kr, HEAD_PAD, MLA_NOPE_DIM) * sn
    scale = (MLA_NOPE_DIM + MLA_ROPE_DIM) ** -0.5 * _LOG2_E
    for h in range(MLA_HEADS):
        sl = slice(h * HEAD_PAD, (h + 1) * HEAD_PAD)
        qh = q[:, sl]
        if rope:
            qh = qh * cs + _rope_partner(qh, HEAD_PAD, MLA_NOPE_DIM) * sn
        q_ref[0, :, sl] = (qh * scale).astype(BF16)
        k_ref[0, :, sl] = (k[:, sl] + kr).astype(BF16)


def _mla_proj(m, qn_w, kvn_w, wq_pad, wk_pad, wv, cos_t, sin_t, rope):
    b, n, w = m.shape
    t = min(512, n)
    hw = MLA_HEADS * HEAD_PAD
    vw = MLA_HEADS * MLA_V_DIM
    const = lambda shape: pl.BlockSpec(shape, lambda i, j: (0, 0))
    tab = pl.BlockSpec((t, HEAD_PAD), lambda i, j: (j, 0))
    return pl.pallas_call(
        functools.partial(_mla_proj_kernel, rope=rope),
        grid=(b, n // t),
        in_specs=[pl.BlockSpec((1, t, w), lambda i, j: (i, j, 0)),
                  const((1, MLA_Q_RANK)), const((1, MLA_KV_RANK)),
                  const((MLA_Q_RANK, hw)), const((MLA_KV_RANK, hw)), const((MLA_KV_RANK, vw)),
                  tab, tab],
        out_specs=[pl.BlockSpec((1, t, hw), lambda i, j: (i, j, 0)),
                   pl.BlockSpec((1, t, hw), lambda i, j: (i, j, 0)),
                   pl.BlockSpec((1, t, vw), lambda i, j: (i, j, 0))],
        out_shape=[jax.ShapeDtypeStruct((b, n, hw), BF16),
                   jax.ShapeDtypeStruct((b, n, hw), BF16),
                   jax.ShapeDtypeStruct((b, n, vw), BF16)],
        compiler_params=_cp(("parallel", "parallel"), VMEM_LIMIT),
        name="mla_proj",
    )(m, qn_w.reshape(1, -1), kvn_w.reshape(1, -1), wq_pad, wk_pad, wv, cos_t, sin_t)


def _attn_kernel(*refs, n_seg):
    q_ref = refs[0]
    k_refs = refs[1:1 + n_seg]
    v_refs = refs[1 + n_seg:1 + 2 * n_seg]
    o_ref = refs[1 + 2 * n_seg]
    tq = q_ref.shape[1]
    lane = lax.broadcasted_iota(jnp.int32, (tq, 2 * MLA_V_DIM), 1)
    for pair in range(MLA_HEADS // 2):
        outs = []
        for h in (2 * pair, 2 * pair + 1):
            sl = slice(h * HEAD_PAD, (h + 1) * HEAD_PAD)
            qh = q_ref[0, :, sl]
            s = [lax.dot_general(qh, kr[0, :, sl], (((1,), (1,)), ((), ())),
                                 preferred_element_type=F32) for kr in k_refs]
            mx = s[0].max(axis=-1, keepdims=True)
            for si in s[1:]:
                mx = jnp.maximum(mx, si.max(axis=-1, keepdims=True))
            den = jnp.zeros((tq, 1), F32)
            acc = jnp.zeros((tq, 2 * MLA_V_DIM), F32)
            for si, vr in zip(s, v_refs):
                p = jnp.exp2(si - mx)
                den = den + p.sum(axis=-1, keepdims=True)
                acc = acc + _dot(p.astype(BF16), vr[0, :, pair * 2 * MLA_V_DIM:(pair + 1) * 2 * MLA_V_DIM])
            outs.append(acc * (1.0 / den))
        o_ref[0, :, pair * 2 * MLA_V_DIM:(pair + 1) * 2 * MLA_V_DIM] = jnp.where(
            lane < MLA_V_DIM, outs[0], outs[1])


def _attention(q, ks, vs):
    b, nq, hw = q.shape
    tq = min(512, nq)
    vw = MLA_HEADS * MLA_V_DIM
    n_seg = len(ks)
    seg_spec = lambda a: pl.BlockSpec((1,) + a.shape[1:], lambda i, j: (i, 0, 0))
    return pl.pallas_call(
        functools.partial(_attn_kernel, n_seg=n_seg),
        grid=(b, nq // tq),
        in_specs=[pl.BlockSpec((1, tq, hw), lambda i, j: (i, j, 0))]
                 + [seg_spec(a) for a in ks] + [seg_spec(a) for a in vs],
        out_specs=pl.BlockSpec((1, tq, vw), lambda i, j: (i, j, 0)),
        out_shape=jax.ShapeDtypeStruct((b, nq, vw), F32),
        compiler_params=_cp(("parallel", "parallel"), VMEM_LIMIT),
        name="mla_attention",
    )(q, *ks, *vs)


def _store_token_tiles(ref, val):
    t = val.shape[0]
    for j in range(SUBLANES):
        ref[pl.ds(j, t, stride=SUBLANES), :] = val[:, j * LANES:(j + 1) * LANES]


def _load_token_tiles(ref, t):
    return jnp.concatenate([ref[pl.ds(j, t, stride=SUBLANES), :] for j in range(SUBLANES)], axis=1)


def _outproj_kernel(cv_ref, rt_ref, ml_ref, x_ref, g1_ref, wo_ref, nw_ref, sh_ref, sc_ref,
                    rwh_ref, rwl_ref, rb_ref, *rest):
    x1_ref, h2_ref, idx_ref, wt_ref = rest[-4:]
    c0, c1 = CONV_CH, CONV_CH + RET_VW
    y = (_dot(cv_ref[0].astype(BF16), wo_ref[:c0, :])
         + _dot(rt_ref[0].astype(BF16), wo_ref[c0:c1, :])
         + _dot(ml_ref[0].astype(BF16), wo_ref[c1:, :]))
    x1 = x_ref[...] + g1_ref[0] * y
    x1_ref[...] = x1
    h2 = _rms_mod(x1, nw_ref[...], sh_ref[0], sc_ref[0])
    _store_token_tiles(h2_ref, h2)
    h_hi, h_lo = _split_bf16(h2)
    nt_dot = lambda a, bm: lax.dot_general(a, bm, (((1,), (1,)), ((), ())), preferred_element_type=F32)
    logits = nt_dot(rwh_ref[...], h_hi) + nt_dot(rwh_ref[...], h_lo) + nt_dot(rwl_ref[...], h_hi)
    scores = _sigmoid(logits)
    sel = scores + rb_ref[...]
    t = scores.shape[1]
    eio = lax.broadcasted_iota(jnp.int32, (N_EXPERTS, t), 0).astype(F32)
    slot = lax.broadcasted_iota(jnp.int32, (SUBLANES, t), 0)
    idx_out = jnp.zeros((SUBLANES, t), jnp.int32)
    wt_out = jnp.zeros((SUBLANES, t), F32)
    wsum = jnp.zeros((1, t), F32)
    for k in range(TOP_K):
        mx = jnp.max(sel, axis=0, keepdims=True)
        ik = jnp.min(jnp.where(sel == mx, eio, float(N_EXPERTS)), axis=0, keepdims=True)
        hit = eio == ik
        wk = jnp.sum(jnp.where(hit, scores, 0.0), axis=0, keepdims=True)
        sel = jnp.where(hit, -jnp.inf, sel)
        idx_out = jnp.where(slot == k, ik.astype(jnp.int32), idx_out)
        wt_out = jnp.where(slot == k, wk, wt_out)
        wsum = wsum + wk
    idx_ref[...] = idx_out
    wt_ref[...] = wt_out / wsum * ROUTED_SCALE


def _outproj(conv, ret, mla, x_flat, x_row0, g1, wo, nw, sh, sc, rw_hi, rw_lo, rb, n_total, row0, carry):
    b, n, _ = conv.shape
    d = x_flat.shape[1]
    x = x_flat
    t = min(256, n)
    nt = n // t
    off = row0 // t
    x_off = x_row0 // t
    tok = lambda w: pl.BlockSpec((1, t, w), lambda i, j: (i, j, 0))
    per_b = pl.BlockSpec((1, 1, d), lambda i, j: (i, 0, 0))
    const = lambda shape: pl.BlockSpec(shape, lambda i, j: (0, 0))
    flat = lambda rows, w: pl.BlockSpec((rows, w), lambda i, j: (off + i * nt + j, 0))
    x_spec = pl.BlockSpec((t, d), lambda i, j: (x_off + i * nt + j, 0))
    in_specs = [tok(CONV_CH), tok(RET_VW), tok(MLA_HEADS * MLA_V_DIM), x_spec, per_b,
                const((d, d)), const((1, d)), per_b, per_b,
                const((N_EXPERTS, d)), const((N_EXPERTS, d)), const((N_EXPERTS, 1))]
    operands = [conv, ret, mla, x, g1, wo, nw.reshape(1, d), sh, sc, rw_hi, rw_lo, rb.reshape(N_EXPERTS, 1)]
    choice = pl.BlockSpec((SUBLANES, t), lambda i, j: (0, off + i * nt + j))
    aliases = {}
    if carry is not None:
        aliases = {len(operands) + k: k for k in range(len(carry))}
        in_specs += [pl.BlockSpec(memory_space=pl.ANY)] * len(carry)
        operands += list(carry)
    return pl.pallas_call(
        _outproj_kernel,
        grid=(b, nt),
        in_specs=in_specs,
        out_specs=[flat(t, d), flat(t * SUBLANES, LANES), choice, choice],
        out_shape=[jax.ShapeDtypeStruct((n_total, d), F32),
                   jax.ShapeDtypeStruct((n_total * SUBLANES, LANES), F32),
                   jax.ShapeDtypeStruct((SUBLANES, n_total), jnp.int32),
                   jax.ShapeDtypeStruct((SUBLANES, n_total), F32)],
        input_output_aliases=aliases,
        compiler_params=_cp(("parallel", "parallel"), VMEM_LIMIT),
        name="outproj_norm2_router",
    )(*operands)


_MOE_ROWS = 32
_MOE_RMW = 16


def _moe_kernel(be_ref, src_ref, cnt_ref, first_ref, nxt_ref, used_ref, tok_ref, wt_ref,
                h_ref, wg_hbm, wu_hbm, wd_hbm, *rest, blocks_per_tile, dump_row, layer):
    o_ref, xa_ref, xb_ref, yt_ref, wg_s, wu_s, wd_s, sem = rest[-8:]
    tile = pl.program_id(0)
    b0 = tile * blocks_per_tile

    def weight_copies(e, s):
        return (pltpu.make_async_copy(wg_hbm.at[layer, e], wg_s.at[s], sem.at[s, 0]),
                pltpu.make_async_copy(wu_hbm.at[layer, e], wu_s.at[s], sem.at[s, 1]),
                pltpu.make_async_copy(wd_hbm.at[layer, e], wd_s.at[s], sem.at[s, 2]))

    def gather_block(b, dst_ref):
        base = src_ref[b]
        for mi in range(MOE_BLOCK):
            t8 = pl.multiple_of(tok_ref[base + mi], SUBLANES)
            dst_ref[mi * SUBLANES:(mi + 1) * SUBLANES, :] = h_ref[0, pl.ds(t8, SUBLANES), :]

    o_ref[...] = jnp.zeros(o_ref.shape, F32)
    n_used = used_ref[tile]

    @pl.when(n_used > 0)
    def _():
        for cp in weight_copies(be_ref[b0], 0):
            cp.start()
        gather_block(b0, xa_ref)

    def block(i, cur_ref, nxt_buf_ref):
        b = b0 + i
        s = first_ref[b] >> 1

        @pl.when((first_ref[b] & 1) == 1)
        def _():
            for cp in weight_copies(be_ref[b], s):
                cp.wait()

            @pl.when(nxt_ref[b] >= 0)
            def _():
                for cp in weight_copies(nxt_ref[b], 1 - s):
                    cp.start()

        cnt = cnt_ref[b]
        base = src_ref[b]
        last = cnt - 1
        x = jnp.concatenate([cur_ref[pl.ds(j, MOE_BLOCK, stride=SUBLANES), :] for j in range(SUBLANES)],
                            axis=1).astype(BF16)
        gather_block(b0 + jnp.minimum(i + 1, n_used - 1), nxt_buf_ref)
        g = _dot(x, wg_s[s])
        u = _dot(x, wu_s[s])
        y = _dot((_silu(g) * u).astype(BF16), wd_s[s])
        for j in range(SUBLANES):
            yt_ref[pl.ds(j * MOE_STRIDE, MOE_BLOCK), :] = y[:, j * LANES:(j + 1) * LANES]
        def scatter_rows(g0, partial):
            for m0 in range(g0, g0 + _MOE_ROWS, _MOE_RMW):
                rows = []
                for mi in range(m0, m0 + _MOE_RMW):
                    if partial:
                        i = base + jnp.minimum(mi, last)
                        t8 = pl.multiple_of(jnp.where(mi < cnt, tok_ref[i], dump_row), SUBLANES)
                    else:
                        i = base + mi
                        t8 = pl.multiple_of(tok_ref[i], SUBLANES)
                    upd = yt_ref[pl.ds(mi, SUBLANES, stride=MOE_STRIDE), :] * wt_ref[i]
                    rows.append((t8, o_ref[0, pl.ds(t8, SUBLANES), :] + upd))
                for t8, val in rows:
                    o_ref[0, pl.ds(t8, SUBLANES), :] = val

        for g0 in range(0, MOE_BLOCK, _MOE_ROWS):
            pl.when(g0 + _MOE_ROWS <= cnt)(functools.partial(scatter_rows, g0, False))
            pl.when((g0 < cnt) & (g0 + _MOE_ROWS > cnt))(functools.partial(scatter_rows, g0, True))

    def pair(i2, carry):
        i = 2 * i2
        block(i, xa_ref, xb_ref)
        pl.when(i + 1 < n_used)(functools.partial(block, i + 1, xb_ref, xa_ref))
        return carry

    lax.fori_loop(0, (n_used + 1) // 2, pair, 0)


def _moe_tile_size(n_tok):
    for s in (4096, 2048, 1024, 512, 256):
        if n_tok % s == 0:
            return s
    raise ValueError(f"token count {n_tok} must be a multiple of 256")


_MOE_SMEM_WORDS = 64 * 1024


def _routed_experts(h3, idx_t, wts_t, wg, wu, wd, layer, ts):
    n_tiles = h3.shape[0]
    d = SUBLANES * LANES
    n_assign = ts * TOP_K
    bpt = n_assign // MOE_BLOCK + N_EXPERTS

    per_tile = lambda a: a[:TOP_K].reshape(TOP_K, n_tiles, ts).transpose(1, 0, 2).reshape(n_tiles, n_assign)
    e_t = per_tile(idx_t)
    tok8 = (jnp.arange(n_assign, dtype=jnp.int32) % ts) * SUBLANES
    tok_bits = (ts * SUBLANES - 1).bit_length()
    s_key, s_w = lax.sort((e_t * (1 << tok_bits) + tok8, per_tile(wts_t)), dimension=1,
                          num_keys=1, is_stable=False)
    s_tok = s_key & ((1 << tok_bits) - 1)
    counts = jnp.sum((e_t[:, :, None] == jnp.arange(N_EXPERTS, dtype=jnp.int32)).astype(jnp.int32), axis=1)
    start = jnp.cumsum(counts, axis=1) - counts
    nblk = (counts + MOE_BLOCK - 1) // MOE_BLOCK
    blk_end = jnp.cumsum(nblk, axis=1)
    blk_start = blk_end - nblk
    used = blk_end[:, -1:]
    bi = jnp.broadcast_to(jnp.arange(bpt, dtype=jnp.int32), (n_tiles, bpt))
    bi_c = jnp.minimum(bi, used - 1)
    e_b = jnp.sum((blk_end[:, None, :] <= bi_c[:, :, None]).astype(jnp.int32), axis=2)
    is_e = e_b[:, :, None] == jnp.arange(N_EXPERTS, dtype=jnp.int32)
    take = lambda a: jnp.sum(jnp.where(is_e, a[:, None, :], 0), axis=2)
    j = bi_c - take(blk_start)
    src = take(start) + j * MOE_BLOCK
    cnt = jnp.clip(take(counts) - j * MOE_BLOCK, 0, MOE_BLOCK)
    e_ids = jnp.arange(N_EXPERTS, dtype=jnp.int32)
    has = nblk > 0
    ordinal = jnp.cumsum(has.astype(jnp.int32), axis=1) - 1
    later = jnp.where(has, e_ids, N_EXPERTS)
    later = jnp.concatenate([later[:, 1:], jnp.full((n_tiles, 1), N_EXPERTS, jnp.int32)], axis=1)
    nxt_e = lax.cummin(later, axis=1, reverse=True)
    nxt_e = jnp.where(nxt_e == N_EXPERTS, -1, nxt_e)
    first = (take(ordinal) % 2) * 2 + (j == 0).astype(jnp.int32)
    nxt = take(nxt_e)

    n_calls = -(-n_tiles // (_MOE_SMEM_WORDS // n_assign))
    bounds = [n_tiles * c // n_calls for c in range(n_calls + 1)]
    out = None
    n_pref = 8
    for c in range(n_calls):
        t0, t1 = bounds[c], bounds[c + 1]
        group = t1 - t0
        src_abs = src[t0:t1] + jnp.arange(group, dtype=jnp.int32)[:, None] * n_assign
        tile_map = lambda i, *_, t0=t0: (t0 + i, 0, 0)
        hbm = pl.BlockSpec(memory_space=pl.ANY)
        in_specs = [pl.BlockSpec((1, ts * SUBLANES, LANES), tile_map, pipeline_mode=pl.Buffered(1)),
                    hbm, hbm, hbm]
        operands = [e_b[t0:t1].reshape(-1), src_abs.reshape(-1), cnt[t0:t1].reshape(-1),
                    first[t0:t1].reshape(-1), nxt[t0:t1].reshape(-1), used[t0:t1].reshape(-1),
                    jnp.pad(s_tok[t0:t1].reshape(-1), (0, MOE_BLOCK)), s_w[t0:t1].reshape(-1), h3, wg, wu, wd]
        aliases = {}
        if out is not None:
            in_specs.append(hbm)
            aliases = {len(operands): 0}
            operands.append(out)
        grid_spec = pltpu.PrefetchScalarGridSpec(
            num_scalar_prefetch=n_pref,
            grid=(group,),
            in_specs=in_specs,
            out_specs=pl.BlockSpec((1, (ts + 1) * SUBLANES, LANES), tile_map, pipeline_mode=pl.Buffered(1)),
            scratch_shapes=[pltpu.VMEM((MOE_BLOCK * SUBLANES, LANES), F32),
                            pltpu.VMEM((MOE_BLOCK * SUBLANES, LANES), F32),
                            pltpu.VMEM((SUBLANES * MOE_STRIDE, LANES), F32),
                            pltpu.VMEM((2, d, EXPERT_DIM), BF16),
                            pltpu.VMEM((2, d, EXPERT_DIM), BF16),
                            pltpu.VMEM((2, EXPERT_DIM, d), BF16),
                            pltpu.SemaphoreType.DMA((2, 3))],
        )
        out = pl.pallas_call(
            functools.partial(_moe_kernel, blocks_per_tile=bpt, dump_row=ts * SUBLANES, layer=layer),
            grid_spec=grid_spec,
            out_shape=jax.ShapeDtypeStruct((n_tiles, (ts + 1) * SUBLANES, LANES), F32),
            input_output_aliases=aliases,
            compiler_params=_cp(("arbitrary",), VMEM_LIMIT),
            name="routed_experts",
        )(*operands)
    return out


def _ffn_out_kernel(x_ref, h_ref, r_ref, g2_ref, sg_ref, su_ref, sd_ref, fw_ref, o_ref, *, final):
    t = x_ref.shape[0]
    h = _load_token_tiles(h_ref.at[0], t).astype(BF16)
    a = _silu(_dot(h, sg_ref[...])) * _dot(h, su_ref[...])
    y = _load_token_tiles(r_ref.at[0], t) + _dot(a.astype(BF16), sd_ref[...])
    x2 = x_ref[...] + g2_ref[0] * y
    if final:
        var = jnp.mean(x2 * x2, axis=-1, keepdims=True)
        x2 = (x2 * lax.rsqrt(var + NORM_EPS)) * fw_ref[...]
    o_ref[...] = x2


def _ffn_out(x1, h3, routed, ts, g2_rows, rows_per_gate, sg, su, sd, fw, final):
    n_tok, d = x1.shape
    t = 512 if (rows_per_gate % 512 == 0 and n_tok % 512 == 0 and ts % 512 == 0) else 256
    tiles_per_row = rows_per_gate // t
    per = ts // t
    last = g2_rows.shape[0] - 1
    tok = pl.BlockSpec((t, d), lambda i: (i, 0))
    tiles = pl.BlockSpec((1, t * SUBLANES, LANES), lambda i: (i // per, i % per, 0))
    const = lambda shape: pl.BlockSpec(shape, lambda i: (0, 0))
    return pl.pallas_call(
        functools.partial(_ffn_out_kernel, final=final),
        grid=(n_tok // t,),
        in_specs=[tok, tiles, tiles,
                  pl.BlockSpec((1, 1, d), lambda i: (jnp.minimum(i // tiles_per_row, last), 0, 0)),
                  const((d, EXPERT_DIM)), const((d, EXPERT_DIM)), const((EXPERT_DIM, d)), const((1, d))],
        out_specs=tok,
        out_shape=jax.ShapeDtypeStruct((n_tok, d), F32),
        compiler_params=_cp(("parallel",), VMEM_LIMIT),
        name="shared_expert_residual",
    )(x1, h3, routed, g2_rows, sg, su, sd, fw.reshape(1, d))


def _rope_tables(rows, dim, group, lo):
    pos_r = jnp.repeat(jnp.arange(rows, dtype=F32), GRID_W)
    pos_c = jnp.tile(jnp.arange(GRID_W, dtype=F32), rows)
    n_freq = dim // 4
    inv = ROPE_BASE ** (-jnp.arange(n_freq, dtype=F32) / n_freq)
    ang = jnp.concatenate([pos_r[:, None] * inv, pos_c[:, None] * inv], axis=-1)
    cos, sin = jnp.cos(ang), jnp.sin(ang)
    n = rows * GRID_W
    half = dim // 2
    cos_g = jnp.ones((n, group), F32).at[:, lo:lo + dim].set(jnp.concatenate([cos, cos], axis=-1))
    sin_g = jnp.zeros((n, group), F32).at[:, lo:lo + dim].set(jnp.concatenate([-sin, sin], axis=-1))
    reps = LANES // group
    return jnp.tile(cos_g, (1, reps)), jnp.tile(sin_g, (1, reps))


def _pad_in_proj(w_in):
    d = w_in.shape[0]
    body = w_in[:, :IN_COLS_PAD - HEAD_PAD]
    kr = w_in[:, IN_COLS_PAD - HEAD_PAD:]
    kr_pad = jnp.zeros((d, HEAD_PAD), w_in.dtype).at[:, MLA_NOPE_DIM:MLA_NOPE_DIM + MLA_ROPE_DIM].set(kr)
    return jnp.concatenate([body, kr_pad], axis=1).astype(BF16)


def _pad_heads(w, width):
    k = w.shape[0]
    w3 = w.reshape(k, MLA_HEADS, width)
    return jnp.zeros((k, MLA_HEADS, HEAD_PAD), w.dtype).at[:, :, :width].set(w3).reshape(
        k, MLA_HEADS * HEAD_PAD).astype(BF16)


def kernel(x, c, ctx, c_ctx, mod_w, mod_b, norm1_w, w_in, conv_w, conv_b, conv_ln_w, conv_ln_b,
           ret_decay_logit, ret_gn_w, q_norm_w, w_uq, kv_norm_w, w_ukv, w_out, norm2_w,
           router_w, router_b, exp_w_gate, exp_w_up, exp_w_down, sh_w_gate, sh_w_up, sh_w_down,
           final_norm_w):
    b, n_lat, d = x.shape
    n_ctx = ctx.shape[1]
    depth = mod_w.shape[0]
    rows = n_lat // GRID_W
    cos_ret, sin_ret = _rope_tables(rows, RET_QK_DIM, RET_QK_DIM, 0)
    cos_mla, sin_mla = _rope_tables(rows, MLA_ROPE_DIM, HEAD_PAD, MLA_NOPE_DIM)

    mod_rows = -(-(b + 1) // SUBLANES) * SUBLANES
    cc = jnp.zeros((mod_rows, d), F32).at[:b].set(c).at[b].set(c_ctx)

    n_l = b * n_lat
    wg_all, wu_all, wd_all = exp_w_gate.astype(BF16), exp_w_up.astype(BF16), exp_w_down.astype(BF16)
    xl, xl_row = x.reshape(n_l, d), 0
    xc, xc_row = ctx.reshape(b * n_ctx, d), 0
    for i in range(depth):
        last = i == depth - 1
        mod = _modulation(cc, mod_w[i], mod_b[i])
        ml = mod[:b].reshape(b, 1, 6, d)
        sh1, sc1, g1, sh2, sc2, g2 = [ml[:, :, j, :] for j in range(6)]
        mc = jnp.broadcast_to(mod[b].reshape(1, 1, 6, d), (b, 1, 6, d))
        csh1, csc1, cg1, csh2, csc2, cg2 = [mc[:, :, j, :] for j in range(6)]

        w_in_p = _pad_in_proj(w_in[i])
        ul, rl, mlat = _inproj(xl, xl_row, b, n_lat, norm1_w[i], sh1, sc1, w_in_p)
        uc, rc, mctx = _inproj(xc, xc_row, b, n_ctx, norm1_w[i], csh1, csc1, w_in_p)

        conv_l = _conv(ul, conv_w[i], conv_b[i], conv_ln_w[i], conv_ln_b[i])
        log_gamma = jax.nn.log_sigmoid(ret_decay_logit[i].astype(F32))
        ret_l, ret_c = _retention(rl, rc, log_gamma, cos_ret, sin_ret, ret_gn_w[i], not last)

        wq_p = _pad_heads(w_uq[i], MLA_NOPE_DIM + MLA_ROPE_DIM)
        wkv = w_ukv[i].reshape(MLA_KV_RANK, MLA_HEADS, MLA_NOPE_DIM + MLA_V_DIM)
        wk_p = _pad_heads(wkv[:, :, :MLA_NOPE_DIM].reshape(MLA_KV_RANK, -1), MLA_NOPE_DIM)
        wv = wkv[:, :, MLA_NOPE_DIM:].reshape(MLA_KV_RANK, -1).astype(BF16)
        ql, kl, vl = _mla_proj(mlat, q_norm_w[i], kv_norm_w[i], wq_p, wk_p, wv, cos_mla, sin_mla, True)
        qc, kc, vc = _mla_proj(mctx, q_norm_w[i], kv_norm_w[i], wq_p, wk_p, wv,
                               cos_mla[:n_ctx], sin_mla[:n_ctx], False)
        mla_l = _attention(ql, [kc, kl], [vc, vl])

        wo = w_out[i].astype(BF16)
        rw_t = router_w[i].T
        rw_hi = rw_t.astype(BF16)
        rw_lo = (rw_t - rw_hi.astype(F32)).astype(BF16)
        n_tok = n_l if last else n_l + b * n_ctx
        outs = _outproj(conv_l, ret_l, mla_l, xl, xl_row, g1, wo, norm2_w[i], sh2, sc2,
                        rw_hi, rw_lo, router_b[i], n_tok, 0, None)
        g2_rows = g2
        if not last:
            conv_c = _conv(uc, conv_w[i], conv_b[i], conv_ln_w[i], conv_ln_b[i])
            mla_c = _attention(qc, [kc], [vc])
            outs = _outproj(conv_c, ret_c, mla_c, xc, xc_row, cg1, wo, norm2_w[i], csh2, csc2,
                            rw_hi, rw_lo, router_b[i], n_tok, n_l, outs)
            g2_rows = jnp.concatenate([g2, cg2[:1]], axis=0)
        x1, h2t, idx, wts = outs

        ts = _moe_tile_size(n_tok)
        h3 = h2t.reshape(n_tok // ts, ts * SUBLANES, LANES)
        routed = _routed_experts(h3, idx, wts, wg_all, wu_all, wd_all, i, ts)
        x2 = _ffn_out(x1, h3, routed, ts, g2_rows, n_lat, sh_w_gate[i].astype(BF16),
                      sh_w_up[i].astype(BF16), sh_w_down[i].astype(BF16), final_norm_w, last)
        xl, xl_row = x2, 0
        xc, xc_row = x2, n_l
    return xl.reshape(b, n_lat, d)
```

```python
import functools

import jax
import jax.numpy as jnp
from jax import lax
from jax.experimental import pallas as pl
from jax.experimental.pallas import tpu as pltpu

F32 = jnp.float32
BF16 = jnp.bfloat16

D_MODEL = 1024
GRID_W = 64
CONV_CH = 256
CONV_K = 31
RET_HEADS = 4
RET_QK_DIM = 32
RET_V_DIM = 64
RET_CHUNK = 128
MLA_HEADS = 8
MLA_NOPE_DIM = 64
MLA_ROPE_DIM = 32
MLA_V_DIM = 64
MLA_Q_RANK = 256
MLA_KV_RANK = 128
ROPE_BASE = 10000.0
N_EXPERTS = 64
TOP_K = 6
EXPERT_DIM = 256
ROUTED_SCALE = 2.5
NORM_EPS = 1e-6
_LOG2_E = 1.4426950408889634

LANES = 128
SUBLANES = 8
HEAD_PAD = 128
RET_W = RET_HEADS * RET_QK_DIM
RET_VW = RET_HEADS * RET_V_DIM
IN_COLS_PAD = 2 * CONV_CH + 2 * RET_W + 2 * RET_VW + MLA_Q_RANK + MLA_KV_RANK + HEAD_PAD
MOE_BLOCK = 256
MOE_STRIDE = MOE_BLOCK + SUBLANES
VMEM_LIMIT = 56 * 1024 * 1024


def _cp(sem, vmem=None):
    return pltpu.CompilerParams(dimension_semantics=sem, vmem_limit_bytes=vmem)


def _dot(a, b):
    return jnp.dot(a, b, preferred_element_type=F32)


def _split_bf16(a):
    hi = a.astype(BF16)
    lo = (a - hi.astype(F32)).astype(BF16)
    return hi, lo


def _sigmoid(x):
    return 1.0 / (1.0 + jnp.exp(-x))


def _silu(x):
    return x * _sigmoid(x)


def _mod_kernel(c_ref, w_ref, b_ref, o_ref):
    a_hi, a_lo = _split_bf16(_silu(c_ref[...]))
    w_hi, w_lo = _split_bf16(w_ref[...])
    o_ref[...] = _dot(a_hi, w_hi) + _dot(a_lo, w_hi) + _dot(a_hi, w_lo) + b_ref[...]


def _modulation(cc, w, b):
    rows, d = cc.shape
    n = w.shape[1]
    bn = 1536
    return pl.pallas_call(
        _mod_kernel,
        grid=(n // bn,),
        in_specs=[pl.BlockSpec((rows, d), lambda j: (0, 0)),
                  pl.BlockSpec((d, bn), lambda j: (0, j)),
                  pl.BlockSpec((1, bn), lambda j: (0, j))],
        out_specs=pl.BlockSpec((rows, bn), lambda j: (0, j)),
        out_shape=jax.ShapeDtypeStruct((rows, n), F32),
        compiler_params=_cp(("arbitrary",), VMEM_LIMIT),
        name="modulation",
    )(cc, w, b.reshape(1, n))


def _rms_mod(x, nw, sh, sc):
    var = jnp.mean(x * x, axis=-1, keepdims=True)
    h = (x * lax.rsqrt(var + NORM_EPS)) * nw
    return h * (1.0 + sc) + sh


def _inproj_kernel(x_ref, nw_ref, sh_ref, sc_ref, w_ref, u_ref, r_ref, m_ref):
    h = _rms_mod(x_ref[...], nw_ref[...], sh_ref[0], sc_ref[0])
    z = _dot(h.astype(BF16), w_ref[...])
    c0 = 2 * CONV_CH
    c1 = c0 + 2 * RET_W + 2 * RET_VW
    u_ref[0] = z[:, :c0]
    r_ref[0] = z[:, c0:c1]
    m_ref[0] = z[:, c1:]


def _inproj(x_flat, row0, b, n, nw, sh, sc, w_pad):
    d = x_flat.shape[1]
    t = min(512, n)
    nt = n // t
    off = row0 // t
    wu, wr, wm = 2 * CONV_CH, 2 * RET_W + 2 * RET_VW, MLA_Q_RANK + MLA_KV_RANK + HEAD_PAD
    tok = lambda w: pl.BlockSpec((1, t, w), lambda i, j: (i, j, 0))
    per_b = pl.BlockSpec((1, 1, d), lambda i, j: (i, 0, 0))
    return pl.pallas_call(
        _inproj_kernel,
        grid=(b, nt),
        in_specs=[pl.BlockSpec((t, d), lambda i, j: (off + i * nt + j, 0)),
                  pl.BlockSpec((1, d), lambda i, j: (0, 0)), per_b, per_b,
                  pl.BlockSpec((d, IN_COLS_PAD), lambda i, j: (0, 0))],
        out_specs=[tok(wu), tok(wr), tok(wm)],
        out_shape=[jax.ShapeDtypeStruct((b, n, wu), F32),
                   jax.ShapeDtypeStruct((b, n, wr), F32),
                   jax.ShapeDtypeStruct((b, n, wm), F32)],
        compiler_params=_cp(("parallel", "parallel"), VMEM_LIMIT),
        name="norm1_inproj",
    )(x_flat, nw.reshape(1, d), sh, sc, w_pad)


_CONV_PAD = 16
_CONV_ROWS = 128


def _conv_kernel(u_ref, cw_ref, cb_ref, lw_ref, lb_ref, o_ref, hp_ref, *, n):
    c = CONV_CH
    hp_ref[0:_CONV_PAD, :] = jnp.zeros((_CONV_PAD, c), F32)
    hp_ref[n + _CONV_PAD:n + 2 * _CONV_PAD, :] = jnp.zeros((_CONV_PAD, c), F32)

    def glu(i, carry):
        r = pl.multiple_of(i * _CONV_ROWS, _CONV_ROWS)
        u = u_ref[0, pl.ds(r, _CONV_ROWS), :]
        hp_ref[pl.ds(r + _CONV_PAD, _CONV_ROWS), :] = u[:, :c] * _sigmoid(u[:, c:])
        return carry

    lax.fori_loop(0, n // _CONV_ROWS, glu, 0)

    def conv(i, carry):
        r = pl.multiple_of(i * _CONV_ROWS, _CONV_ROWS)
        base = _CONV_PAD - CONV_K // 2
        n_q = (base + CONV_K - 1) // SUBLANES + 1
        acc = jnp.zeros((_CONV_ROWS, c), F32)
        for s in range(SUBLANES):
            part = None
            for q in range(n_q):
                k = q * SUBLANES + s - base
                if 0 <= k < CONV_K:
                    win = hp_ref[pl.ds(r + q * SUBLANES, _CONV_ROWS + SUBLANES), :]
                    term = cw_ref[k:k + 1, :] * win
                    part = term if part is None else part + term
            acc = acc + part[s:s + _CONV_ROWS, :]
        hcv = acc + cb_ref[...]
        mu = jnp.mean(hcv, axis=-1, keepdims=True)
        dlt = hcv - mu
        var = jnp.mean(dlt * dlt, axis=-1, keepdims=True)
        y = (dlt * lax.rsqrt(var + NORM_EPS)) * lw_ref[...] + lb_ref[...]
        o_ref[0, pl.ds(r, _CONV_ROWS), :] = _silu(y)
        return carry

    lax.fori_loop(0, n // _CONV_ROWS, conv, 0)


def _conv(u, cw, cb, lw, lb):
    b, n, _ = u.shape
    c = CONV_CH
    vec = pl.BlockSpec((1, c), lambda i: (0, 0))
    return pl.pallas_call(
        functools.partial(_conv_kernel, n=n),
        grid=(b,),
        in_specs=[pl.BlockSpec((1, n, 2 * c), lambda i: (i, 0, 0)),
                  pl.BlockSpec((CONV_K, c), lambda i: (0, 0)), vec, vec, vec],
        out_specs=pl.BlockSpec((1, n, c), lambda i: (i, 0, 0)),
        out_shape=jax.ShapeDtypeStruct((b, n, c), F32),
        scratch_shapes=[pltpu.VMEM((n + 2 * _CONV_PAD, c), F32)],
        compiler_params=_cp(("parallel",), VMEM_LIMIT),
        name="conformer_conv",
    )(u, cw, cb.reshape(1, c), lw.reshape(1, c), lb.reshape(1, c))


def _rope_partner(x, group, lo):
    half = 16
    lane = lax.broadcasted_iota(jnp.int32, x.shape, 1) % group
    first = (lane >= lo) & (lane < lo + half)
    return jnp.where(first, pltpu.roll(x, LANES - half, 1), pltpu.roll(x, half, 1))


def _ret_kernel(lg_ref, rl_ref, rc_ref, cos_ref, sin_ref, gn_ref, avg_ref, ol_ref, oc_ref,
                q_s, k_s, yl_s, yc_s, dst_s, dq_s, dk_s, dch_s, *, n_lat, n_ctx, ctx_out):
    ch = RET_CHUNK
    lane_q = lax.broadcasted_iota(jnp.int32, (1, RET_W), 1) // RET_QK_DIM
    lane_v = lax.broadcasted_iota(jnp.int32, (1, RET_VW), 1) // RET_V_DIM
    row_h = lax.broadcasted_iota(jnp.int32, (RET_W, 1), 0) // RET_QK_DIM
    bd = (row_h == lane_v).astype(F32)
    qmask = [(lane_q == h).astype(F32) for h in range(RET_HEADS)]
    vmask = [(lane_v == h).astype(F32) for h in range(RET_HEADS)]

    ri = lax.broadcasted_iota(jnp.int32, (ch, ch), 0).astype(F32)
    ci = lax.broadcasted_iota(jnp.int32, (ch, ch), 1).astype(F32)
    rowi = lax.broadcasted_iota(jnp.int32, (ch, 1), 0).astype(F32)
    for d in range(2):
        lg_lane = jnp.zeros((1, RET_W), F32)
        lg_row = jnp.zeros((RET_W, 1), F32)
        for h in range(RET_HEADS):
            lg = lg_ref[d * RET_HEADS + h]
            lg_lane = jnp.where(lane_q == h, lg, lg_lane)
            lg_row = jnp.where(row_h == h, lg, lg_row)
            rel = (ri - ci) if d == 0 else (ci - ri)
            dst_s[d, h * ch:(h + 1) * ch, :] = jnp.where(
                rel >= 0, jnp.exp(lg * jnp.maximum(rel, 0.0)), 0.0)
        if d == 0:
            dq_s[d] = jnp.exp(lg_lane * (rowi + 1.0))
            dk_s[d] = jnp.exp(lg_lane * (ch - 1.0 - rowi))
        else:
            dq_s[d] = jnp.exp(lg_lane * (ch - rowi))
            dk_s[d] = jnp.exp(lg_lane * rowi)
        dch_s[d] = jnp.exp(lg_row * float(ch)) * jnp.ones((1, RET_VW), F32)

    kscale = RET_QK_DIM ** -0.5

    def stage(src_ref, n, rope):
        def body(i, carry):
            r = pl.multiple_of(i * ch, ch)
            q = src_ref[0, pl.ds(r, ch), 0:RET_W]
            k = src_ref[0, pl.ds(r, ch), RET_W:2 * RET_W] * kscale
            if rope:
                cs = cos_ref[pl.ds(r, ch), :]
                sn = sin_ref[pl.ds(r, ch), :]
                q = q * cs + _rope_partner(q, RET_QK_DIM, 0) * sn
                k = k * cs + _rope_partner(k, RET_QK_DIM, 0) * sn
            q_s[pl.ds(r, ch), :] = q
            k_s[pl.ds(r, ch), :] = k
            return carry
        lax.fori_loop(0, n // ch, body, 0)

    def scan(src_ref, y_ref, n, s0_fwd, s0_bwd):
        nchunks = n // ch

        def one(d, c, s):
            r = pl.multiple_of(c * ch, ch)
            qc = q_s[pl.ds(r, ch), :]
            kc = k_s[pl.ds(r, ch), :]
            vc = src_ref[0, pl.ds(r, ch), 2 * RET_W:2 * RET_W + RET_VW].astype(BF16)
            qst = jnp.concatenate([qc * qmask[h] for h in range(RET_HEADS)], axis=0).astype(BF16)
            inner = lax.dot_general(qst, kc.astype(BF16), (((1,), (1,)), ((), ())),
                                    preferred_element_type=F32) * dst_s[d]
            o = _dot(inner.astype(BF16), vc)
            y = _dot((qc * dq_s[d]).astype(BF16), s.astype(BF16))
            for h in range(RET_HEADS):
                y = y + o[h * ch:(h + 1) * ch, :] * vmask[h]
            kd_t = (kc * dk_s[d]).T.astype(BF16)
            s_new = s * dch_s[d] + _dot(kd_t, vc) * bd
            y_ref[d, pl.ds(r, ch), :] = y
            return s_new

        def body(i, carry):
            return one(0, i, carry[0]), one(1, nchunks - 1 - i, carry[1])

        return lax.fori_loop(0, nchunks, body, (s0_fwd, s0_bwd))

    def finish(src_ref, y_ref, out_ref, n):
        rows = 4 * ch if n % (4 * ch) == 0 else ch

        def body(i, carry):
            r = pl.multiple_of(i * rows, rows)
            y = y_ref[0, pl.ds(r, rows), :] + y_ref[1, pl.ds(r, rows), :]
            y_hi, y_lo = _split_bf16(y)
            mu = _dot(y_hi, avg_ref[...]) + _dot(y_lo, avg_ref[...])
            dlt = y - mu
            d_hi, d_lo = _split_bf16(dlt * dlt)
            var = _dot(d_hi, avg_ref[...]) + _dot(d_lo, avg_ref[...])
            yn = dlt * lax.rsqrt(var + NORM_EPS)
            g = src_ref[0, pl.ds(r, rows), 2 * RET_W + RET_VW:2 * RET_W + 2 * RET_VW]
            out_ref[0, pl.ds(r, rows), :] = _silu(g) * (yn * gn_ref[...])
            return carry
        lax.fori_loop(0, n // rows, body, 0)

    s_zero = jnp.zeros((RET_W, RET_VW), F32)
    stage(rc_ref, n_ctx, False)
    sc_f, sc_b = scan(rc_ref, yc_s, n_ctx, s_zero, s_zero)
    if ctx_out:
        finish(rc_ref, yc_s, oc_ref, n_ctx)
    else:
        oc_ref[...] = jnp.zeros(oc_ref.shape, F32)
    stage(rl_ref, n_lat, True)
    scan(rl_ref, yl_s, n_lat, sc_f, sc_b)
    finish(rl_ref, yl_s, ol_ref, n_lat)


def _retention(r_lat, r_ctx, log_gamma, cos_t, sin_t, gn_w, ctx_out):
    b, n_lat, w = r_lat.shape
    n_ctx = r_ctx.shape[1]
    grp = jnp.arange(RET_VW) // RET_V_DIM
    avg = ((grp[:, None] == grp[None, :]).astype(F32) / RET_V_DIM).astype(BF16)
    tab = pl.BlockSpec((n_lat, RET_W), lambda i, lg: (0, 0))
    grid_spec = pltpu.PrefetchScalarGridSpec(
        num_scalar_prefetch=1,
        grid=(b,),
        in_specs=[pl.BlockSpec((1, n_lat, w), lambda i, lg: (i, 0, 0)),
                  pl.BlockSpec((1, n_ctx, w), lambda i, lg: (i, 0, 0)),
                  tab, tab,
                  pl.BlockSpec((1, RET_VW), lambda i, lg: (0, 0)),
                  pl.BlockSpec((RET_VW, RET_VW), lambda i, lg: (0, 0))],
        out_specs=[pl.BlockSpec((1, n_lat, RET_VW), lambda i, lg: (i, 0, 0)),
                   pl.BlockSpec((1, n_ctx, RET_VW), lambda i, lg: (i, 0, 0))],
        scratch_shapes=[pltpu.VMEM((max(n_lat, n_ctx), RET_W), F32),
                        pltpu.VMEM((max(n_lat, n_ctx), RET_W), F32),
                        pltpu.VMEM((2, n_lat, RET_VW), F32),
                        pltpu.VMEM((2, n_ctx, RET_VW), F32),
                        pltpu.VMEM((2, RET_HEADS * RET_CHUNK, RET_CHUNK), F32),
                        pltpu.VMEM((2, RET_CHUNK, RET_W), F32),
                        pltpu.VMEM((2, RET_CHUNK, RET_W), F32),
                        pltpu.VMEM((2, RET_W, RET_VW), F32)],
    )
    return pl.pallas_call(
        functools.partial(_ret_kernel, n_lat=n_lat, n_ctx=n_ctx, ctx_out=ctx_out),
        grid_spec=grid_spec,
        out_shape=[jax.ShapeDtypeStruct((b, n_lat, RET_VW), F32),
                   jax.ShapeDtypeStruct((b, n_ctx, RET_VW), F32)],
        compiler_params=_cp(("parallel",), VMEM_LIMIT),
        name="retention",
    )(log_gamma.reshape(-1), r_lat, r_ctx, cos_t, sin_t, gn_w.reshape(1, RET_VW), avg)


def _mla_proj_kernel(m_ref, qn_ref, kvn_ref, wq_ref, wk_ref, wv_ref, cos_ref, sin_ref,
                     q_ref, k_ref, v_ref, *, rope):
    m = m_ref[0]
    cq = m[:, :MLA_Q_RANK]
    ckv = m[:, MLA_Q_RANK:MLA_Q_RANK + MLA_KV_RANK]
    kr = m[:, MLA_Q_RANK + MLA_KV_RANK:]

    def rms(x, w):
        var = jnp.mean(x * x, axis=-1, keepdims=True)
        return (x * lax.rsqrt(var + NORM_EPS)) * w

    q = _dot(rms(cq, qn_ref[...]).astype(BF16), wq_ref[...])
    ckv_n = rms(ckv, kvn_ref[...]).astype(BF16)
    k = _dot(ckv_n, wk_ref[...])
    v_ref[0] = _dot(ckv_n, wv_ref[...]).astype(BF16)
    if rope:
        cs = cos_ref[...]
        sn = sin_ref[...]
        kr = kr * cs + _rope_partner(kr, HEAD_PAD, MLA_NOPE_DIM) * sn
    scale = (MLA_NOPE_DIM + MLA_ROPE_DIM) ** -0.5 * _LOG2_E
    for h in range(MLA_HEADS):
        sl = slice(h * HEAD_PAD, (h + 1) * HEAD_PAD)
        qh = q[:, sl]
        if rope:
            qh = qh * cs + _rope_partner(qh, HEAD_PAD, MLA_NOPE_DIM) * sn
        q_ref[0, :, sl] = (qh * scale).astype(BF16)
        k_ref[0, :, sl] = (k[:, sl] + kr).astype(BF16)


def _mla_proj(m, qn_w, kvn_w, wq_pad, wk_pad, wv, cos_t, sin_t, rope):
    b, n, w = m.shape
    t = min(512, n)
    hw = MLA_HEADS * HEAD_PAD
    vw = MLA_HEADS * MLA_V_DIM
    const = lambda shape: pl.BlockSpec(shape, lambda i, j: (0, 0))
    tab = pl.BlockSpec((t, HEAD_PAD), lambda i, j: (j, 0))
    return pl.pallas_call(
        functools.partial(_mla_proj_kernel, rope=rope),
        grid=(b, n // t),
        in_specs=[pl.BlockSpec((1, t, w), lambda i, j: (i, j, 0)),
                  const((1, MLA_Q_RANK)), const((1, MLA_KV_RANK)),
                  const((MLA_Q_RANK, hw)), const((MLA_KV_RANK, hw)), const((MLA_KV_RANK, vw)),
                  tab, tab],
        out_specs=[pl.BlockSpec((1, t, hw), lambda i, j: (i, j, 0)),
                   pl.BlockSpec((1, t, hw), lambda i, j: (i, j, 0)),
                   pl.BlockSpec((1, t, vw), lambda i, j: (i, j, 0))],
        out_shape=[jax.ShapeDtypeStruct((b, n, hw), BF16),
                   jax.ShapeDtypeStruct((b, n, hw), BF16),
                   jax.ShapeDtypeStruct((b, n, vw), BF16)],
        compiler_params=_cp(("parallel", "parallel"), VMEM_LIMIT),
        name="mla_proj",
    )(m, qn_w.reshape(1, -1), kvn_w.reshape(1, -1), wq_pad, wk_pad, wv, cos_t, sin_t)


def _attn_kernel(*refs, n_seg):
    q_ref = refs[0]
    k_refs = refs[1:1 + n_seg]
    v_refs = refs[1 + n_seg:1 + 2 * n_seg]
    o_ref = refs[1 + 2 * n_seg]
    tq = q_ref.shape[1]
    lane = lax.broadcasted_iota(jnp.int32, (tq, 2 * MLA_V_DIM), 1)
    for pair in range(MLA_HEADS // 2):
        outs = []
        for h in (2 * pair, 2 * pair + 1):
            sl = slice(h * HEAD_PAD, (h + 1) * HEAD_PAD)
            qh = q_ref[0, :, sl]
            s = [lax.dot_general(qh, kr[0, :, sl], (((1,), (1,)), ((), ())),
                                 preferred_element_type=F32) for kr in k_refs]
            mx = s[0].max(axis=-1, keepdims=True)
            for si in s[1:]:
                mx = jnp.maximum(mx, si.max(axis=-1, keepdims=True))
            den = jnp.zeros((tq, 1), F32)
            acc = jnp.zeros((tq, 2 * MLA_V_DIM), F32)
            for si, vr in zip(s, v_refs):
                p = jnp.exp2(si - mx)
                den = den + p.sum(axis=-1, keepdims=True)
                acc = acc + _dot(p.astype(BF16), vr[0, :, pair * 2 * MLA_V_DIM:(pair + 1) * 2 * MLA_V_DIM])
            outs.append(acc * (1.0 / den))
        o_ref[0, :, pair * 2 * MLA_V_DIM:(pair + 1) * 2 * MLA_V_DIM] = jnp.where(
            lane < MLA_V_DIM, outs[0], outs[1])


def _attention(q, ks, vs):
    b, nq, hw = q.shape
    tq = min(512, nq)
    vw = MLA_HEADS * MLA_V_DIM
    n_seg = len(ks)
    seg_spec = lambda a: pl.BlockSpec((1,) + a.shape[1:], lambda i, j: (i, 0, 0))
    return pl.pallas_call(
        functools.partial(_attn_kernel, n_seg=n_seg),
        grid=(b, nq // tq),
        in_specs=[pl.BlockSpec((1, tq, hw), lambda i, j: (i, j, 0))]
                 + [seg_spec(a) for a in ks] + [seg_spec(a) for a in vs],
        out_specs=pl.BlockSpec((1, tq, vw), lambda i, j: (i, j, 0)),
        out_shape=jax.ShapeDtypeStruct((b, nq, vw), F32),
        compiler_params=_cp(("parallel", "parallel"), VMEM_LIMIT),
        name="mla_attention",
    )(q, *ks, *vs)


def _store_token_tiles(ref, val):
    t = val.shape[0]
    for j in range(SUBLANES):
        ref[pl.ds(j, t, stride=SUBLANES), :] = val[:, j * LANES:(j + 1) * LANES]


def _load_token_tiles(ref, t):
    return jnp.concatenate([ref[pl.ds(j, t, stride=SUBLANES), :] for j in range(SUBLANES)], axis=1)


def _outproj_kernel(cv_ref, rt_ref, ml_ref, x_ref, g1_ref, wo_ref, nw_ref, sh_ref, sc_ref,
                    rwh_ref, rwl_ref, rb_ref, *rest):
    x1_ref, h2_ref, idx_ref, wt_ref = rest[-4:]
    c0, c1 = CONV_CH, CONV_CH + RET_VW
    y = (_dot(cv_ref[0].astype(BF16), wo_ref[:c0, :])
         + _dot(rt_ref[0].astype(BF16), wo_ref[c0:c1, :])
         + _dot(ml_ref[0].astype(BF16), wo_ref[c1:, :]))
    x1 = x_ref[...] + g1_ref[0] * y
    x1_ref[...] = x1
    h2 = _rms_mod(x1, nw_ref[...], sh_ref[0], sc_ref[0])
    _store_token_tiles(h2_ref, h2)
    h_hi, h_lo = _split_bf16(h2)
    nt_dot = lambda a, bm: lax.dot_general(a, bm, (((1,), (1,)), ((), ())), preferred_element_type=F32)
    logits = nt_dot(rwh_ref[...], h_hi) + nt_dot(rwh_ref[...], h_lo) + nt_dot(rwl_ref[...], h_hi)
    scores = _sigmoid(logits)
    sel = scores + rb_ref[...]
    t = scores.shape[1]
    eio = lax.broadcasted_iota(jnp.int32, (N_EXPERTS, t), 0).astype(F32)
    slot = lax.broadcasted_iota(jnp.int32, (SUBLANES, t), 0)
    idx_out = jnp.zeros((SUBLANES, t), jnp.int32)
    wt_out = jnp.zeros((SUBLANES, t), F32)
    wsum = jnp.zeros((1, t), F32)
    for k in range(TOP_K):
        mx = jnp.max(sel, axis=0, keepdims=True)
        ik = jnp.min(jnp.where(sel == mx, eio, float(N_EXPERTS)), axis=0, keepdims=True)
        hit = eio == ik
        wk = jnp.sum(jnp.where(hit, scores, 0.0), axis=0, keepdims=True)
        sel = jnp.where(hit, -jnp.inf, sel)
        idx_out = jnp.where(slot == k, ik.astype(jnp.int32), idx_out)
        wt_out = jnp.where(slot == k, wk, wt_out)
        wsum = wsum + wk
    idx_ref[...] = idx_out
    wt_ref[...] = wt_out / wsum * ROUTED_SCALE


def _outproj(conv, ret, mla, x_flat, x_row0, g1, wo, nw, sh, sc, rw_hi, rw_lo, rb, n_total, row0, carry):
    b, n, _ = conv.shape
    d = x_flat.shape[1]
    x = x_flat
    t = min(512, n)
    nt = n // t
    off = row0 // t
    x_off = x_row0 // t
    tok = lambda w: pl.BlockSpec((1, t, w), lambda i, j: (i, j, 0))
    per_b = pl.BlockSpec((1, 1, d), lambda i, j: (i, 0, 0))
    const = lambda shape: pl.BlockSpec(shape, lambda i, j: (0, 0))
    flat = lambda rows, w: pl.BlockSpec((rows, w), lambda i, j: (off + i * nt + j, 0))
    x_spec = pl.BlockSpec((t, d), lambda i, j: (x_off + i * nt + j, 0))
    in_specs = [tok(CONV_CH), tok(RET_VW), tok(MLA_HEADS * MLA_V_DIM), x_spec, per_b,
                const((d, d)), const((1, d)), per_b, per_b,
                const((N_EXPERTS, d)), const((N_EXPERTS, d)), const((N_EXPERTS, 1))]
    operands = [conv, ret, mla, x, g1, wo, nw.reshape(1, d), sh, sc, rw_hi, rw_lo, rb.reshape(N_EXPERTS, 1)]
    choice = pl.BlockSpec((SUBLANES, t), lambda i, j: (0, off + i * nt + j))
    aliases = {}
    if carry is not None:
        aliases = {len(operands) + k: k for k in range(len(carry))}
        in_specs += [pl.BlockSpec(memory_space=pl.ANY)] * len(carry)
        operands += list(carry)
    return pl.pallas_call(
        _outproj_kernel,
        grid=(b, nt),
        in_specs=in_specs,
        out_specs=[flat(t, d), flat(t * SUBLANES, LANES), choice, choice],
        out_shape=[jax.ShapeDtypeStruct((n_total, d), F32),
                   jax.ShapeDtypeStruct((n_total * SUBLANES, LANES), F32),
                   jax.ShapeDtypeStruct((SUBLANES, n_total), jnp.int32),
                   jax.ShapeDtypeStruct((SUBLANES, n_total), F32)],
        input_output_aliases=aliases,
        compiler_params=_cp(("parallel", "parallel"), VMEM_LIMIT),
        name="outproj_norm2_router",
    )(*operands)


_MOE_ROWS = 64
_MOE_RMW = 16


def _moe_kernel(be_ref, src_ref, cnt_ref, first_ref, nxt_ref, used_ref, tok_ref, wt_ref,
                h_ref, wg_hbm, wu_hbm, wd_hbm, *rest, blocks_per_tile, dump_row, layer):
    o_ref, xa_ref, xb_ref, yt_ref, wg_s, wu_s, wd_s, sem = rest[-8:]
    tile = pl.program_id(0)
    b0 = tile * blocks_per_tile

    def weight_copies(e, s):
        return (pltpu.make_async_copy(wg_hbm.at[layer, e], wg_s.at[s], sem.at[s, 0]),
                pltpu.make_async_copy(wu_hbm.at[layer, e], wu_s.at[s], sem.at[s, 1]),
                pltpu.make_async_copy(wd_hbm.at[layer, e], wd_s.at[s], sem.at[s, 2]))

    def gather_block(b, dst_ref):
        base = src_ref[b]
        for mi in range(MOE_BLOCK):
            t8 = pl.multiple_of(tok_ref[base + mi], SUBLANES)
            dst_ref[mi * SUBLANES:(mi + 1) * SUBLANES, :] = h_ref[0, pl.ds(t8, SUBLANES), :]

    o_ref[...] = jnp.zeros(o_ref.shape, F32)
    n_used = used_ref[tile]

    @pl.when(n_used > 0)
    def _():
        for cp in weight_copies(be_ref[b0], 0):
            cp.start()
        gather_block(b0, xa_ref)

    def block(i, cur_ref, nxt_buf_ref):
        b = b0 + i
        s = first_ref[b] >> 1

        @pl.when((first_ref[b] & 1) == 1)
        def _():
            for cp in weight_copies(be_ref[b], s):
                cp.wait()

            @pl.when(nxt_ref[b] >= 0)
            def _():
                for cp in weight_copies(nxt_ref[b], 1 - s):
                    cp.start()

        cnt = cnt_ref[b]
        base = src_ref[b]
        last = cnt - 1
        x = jnp.concatenate([cur_ref[pl.ds(j, MOE_BLOCK, stride=SUBLANES), :] for j in range(SUBLANES)],
                            axis=1).astype(BF16)
        gather_block(b0 + jnp.minimum(i + 1, n_used - 1), nxt_buf_ref)
        g = _dot(x, wg_s[s])
        u = _dot(x, wu_s[s])
        y = _dot((_silu(g) * u).astype(BF16), wd_s[s])
        for j in range(SUBLANES):
            yt_ref[pl.ds(j * MOE_STRIDE, MOE_BLOCK), :] = y[:, j * LANES:(j + 1) * LANES]
        def scatter_rows(g0, partial):
            for m0 in range(g0, g0 + _MOE_ROWS, _MOE_RMW):
                rows = []
                for mi in range(m0, m0 + _MOE_RMW):
                    if partial:
                        i = base + jnp.minimum(mi, last)
                        t8 = pl.multiple_of(jnp.where(mi < cnt, tok_ref[i], dump_row), SUBLANES)
                    else:
                        i = base + mi
                        t8 = pl.multiple_of(tok_ref[i], SUBLANES)
                    upd = yt_ref[pl.ds(mi, SUBLANES, stride=MOE_STRIDE), :] * wt_ref[i]
                    rows.append((t8, o_ref[0, pl.ds(t8, SUBLANES), :] + upd))
                for t8, val in rows:
                    o_ref[0, pl.ds(t8, SUBLANES), :] = val

        for g0 in range(0, MOE_BLOCK, _MOE_ROWS):
            pl.when(g0 + _MOE_ROWS <= cnt)(functools.partial(scatter_rows, g0, False))
            pl.when((g0 < cnt) & (g0 + _MOE_ROWS > cnt))(functools.partial(scatter_rows, g0, True))

    def pair(i2, carry):
        i = 2 * i2
        block(i, xa_ref, xb_ref)
        pl.when(i + 1 < n_used)(functools.partial(block, i + 1, xb_ref, xa_ref))
        return carry

    lax.fori_loop(0, (n_used + 1) // 2, pair, 0)


def _moe_tile_size(n_tok):
    for s in (4096, 2048, 1024, 512, 256):
        if n_tok % s == 0:
            return s
    raise ValueError(f"token count {n_tok} must be a multiple of 256")


_MOE_SMEM_WORDS = 64 * 1024


def _routed_experts(h3, idx_t, wts_t, wg, wu, wd, layer, ts):
    n_tiles = h3.shape[0]
    d = SUBLANES * LANES
    n_assign = ts * TOP_K
    bpt = n_assign // MOE_BLOCK + N_EXPERTS

    per_tile = lambda a: a[:TOP_K].reshape(TOP_K, n_tiles, ts).transpose(1, 0, 2).reshape(n_tiles, n_assign)
    e_t = per_tile(idx_t)
    tok8 = (jnp.arange(n_assign, dtype=jnp.int32) % ts) * SUBLANES
    tok_bits = (ts * SUBLANES - 1).bit_length()
    s_key, s_w = lax.sort((e_t * (1 << tok_bits) + tok8, per_tile(wts_t)), dimension=1,
                          num_keys=1, is_stable=False)
    s_tok = s_key & ((1 << tok_bits) - 1)
    counts = jnp.sum((e_t[:, :, None] == jnp.arange(N_EXPERTS, dtype=jnp.int32)).astype(jnp.int32), axis=1)
    start = jnp.cumsum(counts, axis=1) - counts
    nblk = (counts + MOE_BLOCK - 1) // MOE_BLOCK
    blk_end = jnp.cumsum(nblk, axis=1)
    blk_start = blk_end - nblk
    used = blk_end[:, -1:]
    bi = jnp.broadcast_to(jnp.arange(bpt, dtype=jnp.int32), (n_tiles, bpt))
    bi_c = jnp.minimum(bi, used - 1)
    e_b = jnp.sum((blk_end[:, None, :] <= bi_c[:, :, None]).astype(jnp.int32), axis=2)
    is_e = e_b[:, :, None] == jnp.arange(N_EXPERTS, dtype=jnp.int32)
    take = lambda a: jnp.sum(jnp.where(is_e, a[:, None, :], 0), axis=2)
    j = bi_c - take(blk_start)
    src = take(start) + j * MOE_BLOCK
    cnt = jnp.clip(take(counts) - j * MOE_BLOCK, 0, MOE_BLOCK)
    e_ids = jnp.arange(N_EXPERTS, dtype=jnp.int32)
    has = nblk > 0
    ordinal = jnp.cumsum(has.astype(jnp.int32), axis=1) - 1
    later = jnp.where(has, e_ids, N_EXPERTS)
    later = jnp.concatenate([later[:, 1:], jnp.full((n_tiles, 1), N_EXPERTS, jnp.int32)], axis=1)
    nxt_e = lax.cummin(later, axis=1, reverse=True)
    nxt_e = jnp.where(nxt_e == N_EXPERTS, -1, nxt_e)
    first = (take(ordinal) % 2) * 2 + (j == 0).astype(jnp.int32)
    nxt = take(nxt_e)

    n_calls = -(-n_tiles // (_MOE_SMEM_WORDS // n_assign))
    bounds = [n_tiles * c // n_calls for c in range(n_calls + 1)]
    out = None
    n_pref = 8
    for c in range(n_calls):
        t0, t1 = bounds[c], bounds[c + 1]
        group = t1 - t0
        src_abs = src[t0:t1] + jnp.arange(group, dtype=jnp.int32)[:, None] * n_assign
        tile_map = lambda i, *_, t0=t0: (t0 + i, 0, 0)
        hbm = pl.BlockSpec(memory_space=pl.ANY)
        in_specs = [pl.BlockSpec((1, ts * SUBLANES, LANES), tile_map, pipeline_mode=pl.Buffered(1)),
                    hbm, hbm, hbm]
        operands = [e_b[t0:t1].reshape(-1), src_abs.reshape(-1), cnt[t0:t1].reshape(-1),
                    first[t0:t1].reshape(-1), nxt[t0:t1].reshape(-1), used[t0:t1].reshape(-1),
                    jnp.pad(s_tok[t0:t1].reshape(-1), (0, MOE_BLOCK)), s_w[t0:t1].reshape(-1), h3, wg, wu, wd]
        aliases = {}
        if out is not None:
            in_specs.append(hbm)
            aliases = {len(operands): 0}
            operands.append(out)
        grid_spec = pltpu.PrefetchScalarGridSpec(
            num_scalar_prefetch=n_pref,
            grid=(group,),
            in_specs=in_specs,
            out_specs=pl.BlockSpec((1, (ts + 1) * SUBLANES, LANES), tile_map, pipeline_mode=pl.Buffered(1)),
            scratch_shapes=[pltpu.VMEM((MOE_BLOCK * SUBLANES, LANES), F32),
                            pltpu.VMEM((MOE_BLOCK * SUBLANES, LANES), F32),
                            pltpu.VMEM((SUBLANES * MOE_STRIDE, LANES), F32),
                            pltpu.VMEM((2, d, EXPERT_DIM), BF16),
                            pltpu.VMEM((2, d, EXPERT_DIM), BF16),
                            pltpu.VMEM((2, EXPERT_DIM, d), BF16),
                            pltpu.SemaphoreType.DMA((2, 3))],
        )
        out = pl.pallas_call(
            functools.partial(_moe_kernel, blocks_per_tile=bpt, dump_row=ts * SUBLANES, layer=layer),
            grid_spec=grid_spec,
            out_shape=jax.ShapeDtypeStruct((n_tiles, (ts + 1) * SUBLANES, LANES), F32),
            input_output_aliases=aliases,
            compiler_params=_cp(("arbitrary",), VMEM_LIMIT),
            name="routed_experts",
        )(*operands)
    return out


def _ffn_out_kernel(x_ref, h_ref, r_ref, g2_ref, sg_ref, su_ref, sd_ref, fw_ref, o_ref, *, final):
    t = x_ref.shape[0]
    h = _load_token_tiles(h_ref.at[0], t).astype(BF16)
    a = _silu(_dot(h, sg_ref[...])) * _dot(h, su_ref[...])
    y = _load_token_tiles(r_ref.at[0], t) + _dot(a.astype(BF16), sd_ref[...])
    x2 = x_ref[...] + g2_ref[0] * y
    if final:
        var = jnp.mean(x2 * x2, axis=-1, keepdims=True)
        x2 = (x2 * lax.rsqrt(var + NORM_EPS)) * fw_ref[...]
    o_ref[...] = x2


def _ffn_out(x1, h3, routed, ts, g2_rows, rows_per_gate, sg, su, sd, fw, final):
    n_tok, d = x1.shape
    t = 512 if (rows_per_gate % 512 == 0 and n_tok % 512 == 0 and ts % 512 == 0) else 256
    tiles_per_row = rows_per_gate // t
    per = ts // t
    last = g2_rows.shape[0] - 1
    tok = pl.BlockSpec((t, d), lambda i: (i, 0))
    tiles = pl.BlockSpec((1, t * SUBLANES, LANES), lambda i: (i // per, i % per, 0))
    const = lambda shape: pl.BlockSpec(shape, lambda i: (0, 0))
    return pl.pallas_call(
        functools.partial(_ffn_out_kernel, final=final),
        grid=(n_tok // t,),
        in_specs=[tok, tiles, tiles,
                  pl.BlockSpec((1, 1, d), lambda i: (jnp.minimum(i // tiles_per_row, last), 0, 0)),
                  const((d, EXPERT_DIM)), const((d, EXPERT_DIM)), const((EXPERT_DIM, d)), const((1, d))],
        out_specs=tok,
        out_shape=jax.ShapeDtypeStruct((n_tok, d), F32),
        compiler_params=_cp(("parallel",), VMEM_LIMIT),
        name="shared_expert_residual",
    )(x1, h3, routed, g2_rows, sg, su, sd, fw.reshape(1, d))


def _rope_tables(rows, dim, group, lo):
    pos_r = jnp.repeat(jnp.arange(rows, dtype=F32), GRID_W)
    pos_c = jnp.tile(jnp.arange(GRID_W, dtype=F32), rows)
    n_freq = dim // 4
    inv = ROPE_BASE ** (-jnp.arange(n_freq, dtype=F32) / n_freq)
    ang = jnp.concatenate([pos_r[:, None] * inv, pos_c[:, None] * inv], axis=-1)
    cos, sin = jnp.cos(ang), jnp.sin(ang)
    n = rows * GRID_W
    half = dim // 2
    cos_g = jnp.ones((n, group), F32).at[:, lo:lo + dim].set(jnp.concatenate([cos, cos], axis=-1))
    sin_g = jnp.zeros((n, group), F32).at[:, lo:lo + dim].set(jnp.concatenate([-sin, sin], axis=-1))
    reps = LANES // group
    return jnp.tile(cos_g, (1, reps)), jnp.tile(sin_g, (1, reps))


def _pad_in_proj(w_in):
    d = w_in.shape[0]
    body = w_in[:, :IN_COLS_PAD - HEAD_PAD]
    kr = w_in[:, IN_COLS_PAD - HEAD_PAD:]
    kr_pad = jnp.zeros((d, HEAD_PAD), w_in.dtype).at[:, MLA_NOPE_DIM:MLA_NOPE_DIM + MLA_ROPE_DIM].set(kr)
    return jnp.concatenate([body, kr_pad], axis=1).astype(BF16)


def _pad_heads(w, width):
    k = w.shape[0]
    w3 = w.reshape(k, MLA_HEADS, width)
    return jnp.zeros((k, MLA_HEADS, HEAD_PAD), w.dtype).at[:, :, :width].set(w3).reshape(
        k, MLA_HEADS * HEAD_PAD).astype(BF16)


def kernel(x, c, ctx, c_ctx, mod_w, mod_b, norm1_w, w_in, conv_w, conv_b, conv_ln_w, conv_ln_b,
           ret_decay_logit, ret_gn_w, q_norm_w, w_uq, kv_norm_w, w_ukv, w_out, norm2_w,
           router_w, router_b, exp_w_gate, exp_w_up, exp_w_down, sh_w_gate, sh_w_up, sh_w_down,
           final_norm_w):
    b, n_lat, d = x.shape
    n_ctx = ctx.shape[1]
    depth = mod_w.shape[0]
    rows = n_lat // GRID_W
    cos_ret, sin_ret = _rope_tables(rows, RET_QK_DIM, RET_QK_DIM, 0)
    cos_mla, sin_mla = _rope_tables(rows, MLA_ROPE_DIM, HEAD_PAD, MLA_NOPE_DIM)

    mod_rows = -(-(b + 1) // SUBLANES) * SUBLANES
    cc = jnp.zeros((mod_rows, d), F32).at[:b].set(c).at[b].set(c_ctx)

    n_l = b * n_lat
    wg_all, wu_all, wd_all = exp_w_gate.astype(BF16), exp_w_up.astype(BF16), exp_w_down.astype(BF16)
    xl, xl_row = x.reshape(n_l, d), 0
    xc, xc_row = ctx.reshape(b * n_ctx, d), 0
    for i in range(depth):
        last = i == depth - 1
        mod = _modulation(cc, mod_w[i], mod_b[i])
        ml = mod[:b].reshape(b, 1, 6, d)
        sh1, sc1, g1, sh2, sc2, g2 = [ml[:, :, j, :] for j in range(6)]
        mc = jnp.broadcast_to(mod[b].reshape(1, 1, 6, d), (b, 1, 6, d))
        csh1, csc1, cg1, csh2, csc2, cg2 = [mc[:, :, j, :] for j in range(6)]

        w_in_p = _pad_in_proj(w_in[i])
        ul, rl, mlat = _inproj(xl, xl_row, b, n_lat, norm1_w[i], sh1, sc1, w_in_p)
        uc, rc, mctx = _inproj(xc, xc_row, b, n_ctx, norm1_w[i], csh1, csc1, w_in_p)

        conv_l = _conv(ul, conv_w[i], conv_b[i], conv_ln_w[i], conv_ln_b[i])
        log_gamma = jax.nn.log_sigmoid(ret_decay_logit[i].astype(F32))
        ret_l, ret_c = _retention(rl, rc, log_gamma, cos_ret, sin_ret, ret_gn_w[i], not last)

        wq_p = _pad_heads(w_uq[i], MLA_NOPE_DIM + MLA_ROPE_DIM)
        wkv = w_ukv[i].reshape(MLA_KV_RANK, MLA_HEADS, MLA_NOPE_DIM + MLA_V_DIM)
        wk_p = _pad_heads(wkv[:, :, :MLA_NOPE_DIM].reshape(MLA_KV_RANK, -1), MLA_NOPE_DIM)
        wv = wkv[:, :, MLA_NOPE_DIM:].reshape(MLA_KV_RANK, -1).astype(BF16)
        ql, kl, vl = _mla_proj(mlat, q_norm_w[i], kv_norm_w[i], wq_p, wk_p, wv, cos_mla, sin_mla, True)
        qc, kc, vc = _mla_proj(mctx, q_norm_w[i], kv_norm_w[i], wq_p, wk_p, wv,
                               cos_mla[:n_ctx], sin_mla[:n_ctx], False)
        mla_l = _attention(ql, [kc, kl], [vc, vl])

        wo = w_out[i].astype(BF16)
        rw_t = router_w[i].T
        rw_hi = rw_t.astype(BF16)
        rw_lo = (rw_t - rw_hi.astype(F32)).astype(BF16)
        n_tok = n_l if last else n_l + b * n_ctx
        outs = _outproj(conv_l, ret_l, mla_l, xl, xl_row, g1, wo, norm2_w[i], sh2, sc2,
                        rw_hi, rw_lo, router_b[i], n_tok, 0, None)
        g2_rows = g2
        if not last:
            conv_c = _conv(uc, conv_w[i], conv_b[i], conv_ln_w[i], conv_ln_b[i])
            mla_c = _attention(qc, [kc], [vc])
            outs = _outproj(conv_c, ret_c, mla_c, xc, xc_row, cg1, wo, norm2_w[i], csh2, csc2,
                            rw_hi, rw_lo, router_b[i], n_tok, n_l, outs)
            g2_rows = jnp.concatenate([g2, cg2[:1]], axis=0)
        x1, h2t, idx, wts = outs

        ts = _moe_tile_size(n_tok)
        h3 = h2t.reshape(n_tok // ts, ts * SUBLANES, LANES)
        routed = _routed_experts(h3, idx, wts, wg_all, wu_all, wd_all, i, ts)
        x2 = _ffn_out(x1, h3, routed, ts, g2_rows, n_lat, sh_w_gate[i].astype(BF16),
                      sh_w_up[i].astype(BF16), sh_w_down[i].astype(BF16), final_norm_w, last)
        xl, xl_row = x2, 0
        xc, xc_row = x2, n_l
    return xl.reshape(b, n_lat, d)
```

```python
import functools

import jax
import jax.numpy as jnp
from jax import lax
from jax.experimental import pallas as pl
from jax.experimental.pallas import tpu as pltpu

F32 = jnp.float32
BF16 = jnp.bfloat16

D_MODEL = 1024
GRID_W = 64
CONV_CH = 256
CONV_K = 31
RET_HEADS = 4
RET_QK_DIM = 32
RET_V_DIM = 64
RET_CHUNK = 128
MLA_HEADS = 8
MLA_NOPE_DIM = 64
MLA_ROPE_DIM = 32
MLA_V_DIM = 64
MLA_Q_RANK = 256
MLA_KV_RANK = 128
ROPE_BASE = 10000.0
N_EXPERTS = 64
TOP_K = 6
EXPERT_DIM = 256
ROUTED_SCALE = 2.5
NORM_EPS = 1e-6
_LOG2_E = 1.4426950408889634

LANES = 128
SUBLANES = 8
HEAD_PAD = 128
RET_W = RET_HEADS * RET_QK_DIM
RET_VW = RET_HEADS * RET_V_DIM
IN_COLS_PAD = 2 * CONV_CH + 2 * RET_W + 2 * RET_VW + MLA_Q_RANK + MLA_KV_RANK + HEAD_PAD
MOE_BLOCK = 256
MOE_STRIDE = MOE_BLOCK + SUBLANES
VMEM_LIMIT = 56 * 1024 * 1024


def _cp(sem, vmem=None):
    return pltpu.CompilerParams(dimension_semantics=sem, vmem_limit_bytes=vmem)


def _dot(a, b):
    return jnp.dot(a, b, preferred_element_type=F32)


def _split_bf16(a):
    hi = a.astype(BF16)
    lo = (a - hi.astype(F32)).astype(BF16)
    return hi, lo


def _sigmoid(x):
    return 1.0 / (1.0 + jnp.exp(-x))


def _silu(x):
    return x * _sigmoid(x)


def _mod_kernel(c_ref, w_ref, b_ref, o_ref):
    a_hi, a_lo = _split_bf16(_silu(c_ref[...]))
    w_hi, w_lo = _split_bf16(w_ref[...])
    o_ref[...] = _dot(a_hi, w_hi) + _dot(a_lo, w_hi) + _dot(a_hi, w_lo) + b_ref[...]


def _modulation(cc, w, b):
    rows, d = cc.shape
    n = w.shape[1]
    bn = 1536
    return pl.pallas_call(
        _mod_kernel,
        grid=(n // bn,),
        in_specs=[pl.BlockSpec((rows, d), lambda j: (0, 0)),
                  pl.BlockSpec((d, bn), lambda j: (0, j)),
                  pl.BlockSpec((1, bn), lambda j: (0, j))],
        out_specs=pl.BlockSpec((rows, bn), lambda j: (0, j)),
        out_shape=jax.ShapeDtypeStruct((rows, n), F32),
        compiler_params=_cp(("arbitrary",), VMEM_LIMIT),
        name="modulation",
    )(cc, w, b.reshape(1, n))


def _rms_mod(x, nw, sh, sc):
    var = jnp.mean(x * x, axis=-1, keepdims=True)
    h = (x * lax.rsqrt(var + NORM_EPS)) * nw
    return h * (1.0 + sc) + sh


def _inproj_kernel(x_ref, nw_ref, sh_ref, sc_ref, w_ref, u_ref, r_ref, m_ref):
    h = _rms_mod(x_ref[...], nw_ref[...], sh_ref[0], sc_ref[0])
    z = _dot(h.astype(BF16), w_ref[...])
    c0 = 2 * CONV_CH
    c1 = c0 + 2 * RET_W + 2 * RET_VW
    u_ref[0] = z[:, :c0]
    r_ref[0] = z[:, c0:c1]
    m_ref[0] = z[:, c1:]


def _inproj(x_flat, row0, b, n, nw, sh, sc, w_pad):
    d = x_flat.shape[1]
    t = min(512, n)
    nt = n // t
    off = row0 // t
    wu, wr, wm = 2 * CONV_CH, 2 * RET_W + 2 * RET_VW, MLA_Q_RANK + MLA_KV_RANK + HEAD_PAD
    tok = lambda w: pl.BlockSpec((1, t, w), lambda i, j: (i, j, 0))
    per_b = pl.BlockSpec((1, 1, d), lambda i, j: (i, 0, 0))
    return pl.pallas_call(
        _inproj_kernel,
        grid=(b, nt),
        in_specs=[pl.BlockSpec((t, d), lambda i, j: (off + i * nt + j, 0)),
                  pl.BlockSpec((1, d), lambda i, j: (0, 0)), per_b, per_b,
                  pl.BlockSpec((d, IN_COLS_PAD), lambda i, j: (0, 0))],
        out_specs=[tok(wu), tok(wr), tok(wm)],
        out_shape=[jax.ShapeDtypeStruct((b, n, wu), F32),
                   jax.ShapeDtypeStruct((b, n, wr), F32),
                   jax.ShapeDtypeStruct((b, n, wm), F32)],
        compiler_params=_cp(("parallel", "parallel"), VMEM_LIMIT),
        name="norm1_inproj",
    )(x_flat, nw.reshape(1, d), sh, sc, w_pad)


_CONV_PAD = 16
_CONV_ROWS = 128


def _conv_kernel(u_ref, cw_ref, cb_ref, lw_ref, lb_ref, o_ref, hp_ref, *, n):
    c = CONV_CH
    hp_ref[0:_CONV_PAD, :] = jnp.zeros((_CONV_PAD, c), F32)
    hp_ref[n + _CONV_PAD:n + 2 * _CONV_PAD, :] = jnp.zeros((_CONV_PAD, c), F32)

    def glu(i, carry):
        r = pl.multiple_of(i * _CONV_ROWS, _CONV_ROWS)
        u = u_ref[0, pl.ds(r, _CONV_ROWS), :]
        hp_ref[pl.ds(r + _CONV_PAD, _CONV_ROWS), :] = u[:, :c] * _sigmoid(u[:, c:])
        return carry

    lax.fori_loop(0, n // _CONV_ROWS, glu, 0)

    def conv(i, carry):
        r = pl.multiple_of(i * _CONV_ROWS, _CONV_ROWS)
        base = _CONV_PAD - CONV_K // 2
        n_q = (base + CONV_K - 1) // SUBLANES + 1
        acc = jnp.zeros((_CONV_ROWS, c), F32)
        for s in range(SUBLANES):
            part = None
            for q in range(n_q):
                k = q * SUBLANES + s - base
                if 0 <= k < CONV_K:
                    win = hp_ref[pl.ds(r + q * SUBLANES, _CONV_ROWS + SUBLANES), :]
                    term = cw_ref[k:k + 1, :] * win
                    part = term if part is None else part + term
            acc = acc + part[s:s + _CONV_ROWS, :]
        hcv = acc + cb_ref[...]
        mu = jnp.mean(hcv, axis=-1, keepdims=True)
        dlt = hcv - mu
        var = jnp.mean(dlt * dlt, axis=-1, keepdims=True)
        y = (dlt * lax.rsqrt(var + NORM_EPS)) * lw_ref[...] + lb_ref[...]
        o_ref[0, pl.ds(r, _CONV_ROWS), :] = _silu(y)
        return carry

    lax.fori_loop(0, n // _CONV_ROWS, conv, 0)


def _conv(u, cw, cb, lw, lb):
    b, n, _ = u.shape
    c = CONV_CH
    vec = pl.BlockSpec((1, c), lambda i: (0, 0))
    return pl.pallas_call(
        functools.partial(_conv_kernel, n=n),
        grid=(b,),
        in_specs=[pl.BlockSpec((1, n, 2 * c), lambda i: (i, 0, 0)),
                  pl.BlockSpec((CONV_K, c), lambda i: (0, 0)), vec, vec, vec],
        out_specs=pl.BlockSpec((1, n, c), lambda i: (i, 0, 0)),
        out_shape=jax.ShapeDtypeStruct((b, n, c), F32),
        scratch_shapes=[pltpu.VMEM((n + 2 * _CONV_PAD, c), F32)],
        compiler_params=_cp(("parallel",), VMEM_LIMIT),
        name="conformer_conv",
    )(u, cw, cb.reshape(1, c), lw.reshape(1, c), lb.reshape(1, c))


def _rope_partner(x, group, lo):
    half = 16
    lane = lax.broadcasted_iota(jnp.int32, x.shape, 1) % group
    first = (lane >= lo) & (lane < lo + half)
    return jnp.where(first, pltpu.roll(x, LANES - half, 1), pltpu.roll(x, half, 1))


def _ret_kernel(lg_ref, rl_ref, rc_ref, cos_ref, sin_ref, gn_ref, avg_ref, ol_ref, oc_ref,
                q_s, k_s, yl_s, yc_s, dst_s, dq_s, dk_s, dch_s, *, n_lat, n_ctx, ctx_out):
    ch = RET_CHUNK
    lane_q = lax.broadcasted_iota(jnp.int32, (1, RET_W), 1) // RET_QK_DIM
    lane_v = lax.broadcasted_iota(jnp.int32, (1, RET_VW), 1) // RET_V_DIM
    row_h = lax.broadcasted_iota(jnp.int32, (RET_W, 1), 0) // RET_QK_DIM
    bd = (row_h == lane_v).astype(F32)
    qmask = [(lane_q == h).astype(F32) for h in range(RET_HEADS)]
    vmask = [(lane_v == h).astype(F32) for h in range(RET_HEADS)]

    ri = lax.broadcasted_iota(jnp.int32, (ch, ch), 0).astype(F32)
    ci = lax.broadcasted_iota(jnp.int32, (ch, ch), 1).astype(F32)
    rowi = lax.broadcasted_iota(jnp.int32, (ch, 1), 0).astype(F32)
    for d in range(2):
        lg_lane = jnp.zeros((1, RET_W), F32)
        lg_row = jnp.zeros((RET_W, 1), F32)
        for h in range(RET_HEADS):
            lg = lg_ref[d * RET_HEADS + h]
            lg_lane = jnp.where(lane_q == h, lg, lg_lane)
            lg_row = jnp.where(row_h == h, lg, lg_row)
            rel = (ri - ci) if d == 0 else (ci - ri)
            dst_s[d, h * ch:(h + 1) * ch, :] = jnp.where(
                rel >= 0, jnp.exp(lg * jnp.maximum(rel, 0.0)), 0.0)
        if d == 0:
            dq_s[d] = jnp.exp(lg_lane * (rowi + 1.0))
            dk_s[d] = jnp.exp(lg_lane * (ch - 1.0 - rowi))
        else:
            dq_s[d] = jnp.exp(lg_lane * (ch - rowi))
            dk_s[d] = jnp.exp(lg_lane * rowi)
        dch_s[d] = jnp.exp(lg_row * float(ch)) * jnp.ones((1, RET_VW), F32)

    kscale = RET_QK_DIM ** -0.5

    def stage(src_ref, n, rope):
        rows = 4 * ch if n % (4 * ch) == 0 else ch

        def body(i, carry):
            r = pl.multiple_of(i * rows, rows)
            q = src_ref[0, pl.ds(r, rows), 0:RET_W]
            k = src_ref[0, pl.ds(r, rows), RET_W:2 * RET_W] * kscale
            if rope:
                cs = cos_ref[pl.ds(r, rows), :]
                sn = sin_ref[pl.ds(r, rows), :]
                q = q * cs + _rope_partner(q, RET_QK_DIM, 0) * sn
                k = k * cs + _rope_partner(k, RET_QK_DIM, 0) * sn
            q_s[pl.ds(r, rows), :] = q
            k_s[pl.ds(r, rows), :] = k
            return carry
        lax.fori_loop(0, n // rows, body, 0)

    def scan(src_ref, y_ref, n, s0_fwd, s0_bwd):
        nchunks = n // ch

        def one(d, c, s):
            r = pl.multiple_of(c * ch, ch)
            qc = q_s[pl.ds(r, ch), :]
            kc = k_s[pl.ds(r, ch), :]
            vc = src_ref[0, pl.ds(r, ch), 2 * RET_W:2 * RET_W + RET_VW].astype(BF16)
            qst = jnp.concatenate([qc * qmask[h] for h in range(RET_HEADS)], axis=0).astype(BF16)
            inner = lax.dot_general(qst, kc.astype(BF16), (((1,), (1,)), ((), ())),
                                    preferred_element_type=F32) * dst_s[d]
            o = _dot(inner.astype(BF16), vc)
            y = _dot((qc * dq_s[d]).astype(BF16), s.astype(BF16))
            for h in range(RET_HEADS):
                y = y + o[h * ch:(h + 1) * ch, :] * vmask[h]
            kd_t = (kc * dk_s[d]).T.astype(BF16)
            s_new = s * dch_s[d] + _dot(kd_t, vc) * bd
            y_ref[d, pl.ds(r, ch), :] = y
            return s_new

        def body(i, carry):
            return one(0, i, carry[0]), one(1, nchunks - 1 - i, carry[1])

        return lax.fori_loop(0, nchunks, body, (s0_fwd, s0_bwd))

    def finish(src_ref, y_ref, out_ref, n):
        rows = 4 * ch if n % (4 * ch) == 0 else ch

        def body(i, carry):
            r = pl.multiple_of(i * rows, rows)
            y = y_ref[0, pl.ds(r, rows), :] + y_ref[1, pl.ds(r, rows), :]
            y_hi, y_lo = _split_bf16(y)
            mu = _dot(y_hi, avg_ref[...]) + _dot(y_lo, avg_ref[...])
            dlt = y - mu
            d_hi, d_lo = _split_bf16(dlt * dlt)
            var = _dot(d_hi, avg_ref[...]) + _dot(d_lo, avg_ref[...])
            yn = dlt * lax.rsqrt(var + NORM_EPS)
            g = src_ref[0, pl.ds(r, rows), 2 * RET_W + RET_VW:2 * RET_W + 2 * RET_VW]
            out_ref[0, pl.ds(r, rows), :] = _silu(g) * (yn * gn_ref[...])
            return carry
        lax.fori_loop(0, n // rows, body, 0)

    s_zero = jnp.zeros((RET_W, RET_VW), F32)
    stage(rc_ref, n_ctx, False)
    sc_f, sc_b = scan(rc_ref, yc_s, n_ctx, s_zero, s_zero)
    if ctx_out:
        finish(rc_ref, yc_s, oc_ref, n_ctx)
    else:
        oc_ref[...] = jnp.zeros(oc_ref.shape, F32)
    stage(rl_ref, n_lat, True)
    scan(rl_ref, yl_s, n_lat, sc_f, sc_b)
    finish(rl_ref, yl_s, ol_ref, n_lat)


def _retention(r_lat, r_ctx, log_gamma, cos_t, sin_t, gn_w, ctx_out):
    b, n_lat, w = r_lat.shape
    n_ctx = r_ctx.shape[1]
    grp = jnp.arange(RET_VW) // RET_V_DIM
    avg = ((grp[:, None] == grp[None, :]).astype(F32) / RET_V_DIM).astype(BF16)
    tab = pl.BlockSpec((n_lat, RET_W), lambda i, lg: (0, 0))
    grid_spec = pltpu.PrefetchScalarGridSpec(
        num_scalar_prefetch=1,
        grid=(b,),
        in_specs=[pl.BlockSpec((1, n_lat, w), lambda i, lg: (i, 0, 0)),
                  pl.BlockSpec((1, n_ctx, w), lambda i, lg: (i, 0, 0)),
                  tab, tab,
                  pl.BlockSpec((1, RET_VW), lambda i, lg: (0, 0)),
                  pl.BlockSpec((RET_VW, RET_VW), lambda i, lg: (0, 0))],
        out_specs=[pl.BlockSpec((1, n_lat, RET_VW), lambda i, lg: (i, 0, 0)),
                   pl.BlockSpec((1, n_ctx, RET_VW), lambda i, lg: (i, 0, 0))],
        scratch_shapes=[pltpu.VMEM((max(n_lat, n_ctx), RET_W), F32),
                        pltpu.VMEM((max(n_lat, n_ctx), RET_W), F32),
                        pltpu.VMEM((2, n_lat, RET_VW), F32),
                        pltpu.VMEM((2, n_ctx, RET_VW), F32),
                        pltpu.VMEM((2, RET_HEADS * RET_CHUNK, RET_CHUNK), F32),
                        pltpu.VMEM((2, RET_CHUNK, RET_W), F32),
                        pltpu.VMEM((2, RET_CHUNK, RET_W), F32),
                        pltpu.VMEM((2, RET_W, RET_VW), F32)],
    )
    return pl.pallas_call(
        functools.partial(_ret_kernel, n_lat=n_lat, n_ctx=n_ctx, ctx_out=ctx_out),
        grid_spec=grid_spec,
        out_shape=[jax.ShapeDtypeStruct((b, n_lat, RET_VW), F32),
                   jax.ShapeDtypeStruct((b, n_ctx, RET_VW), F32)],
        compiler_params=_cp(("parallel",), VMEM_LIMIT),
        name="retention",
    )(log_gamma.reshape(-1), r_lat, r_ctx, cos_t, sin_t, gn_w.reshape(1, RET_VW), avg)


def _mla_proj_kernel(m_ref, qn_ref, kvn_ref, wq_ref, wk_ref, wv_ref, cos_ref, sin_ref,
                     q_ref, k_ref, v_ref, *, rope):
    m = m_ref[0]
    cq = m[:, :MLA_Q_RANK]
    ckv = m[:, MLA_Q_RANK:MLA_Q_RANK + MLA_KV_RANK]
    kr = m[:, MLA_Q_RANK + MLA_KV_RANK:]

    def rms(x, w):
        var = jnp.mean(x * x, axis=-1, keepdims=True)
        return (x * lax.rsqrt(var + NORM_EPS)) * w

    q = _dot(rms(cq, qn_ref[...]).astype(BF16), wq_ref[...])
    ckv_n = rms(ckv, kvn_ref[...]).astype(BF16)
    k = _dot(ckv_n, wk_ref[...])
    v_ref[0] = _dot(ckv_n, wv_ref[...]).astype(BF16)
    if rope:
        cs = cos_ref[...]
        sn = sin_ref[...]
        kr = kr * cs + _rope_partner(kr, HEAD_PAD, MLA_NOPE_DIM) * sn
    scale = (MLA_NOPE_DIM + MLA_ROPE_DIM) ** -0.5 * _LOG2_E
    for h in range(MLA_HEADS):
        sl = slice(h * HEAD_PAD, (h + 1) * HEAD_PAD)
        qh = q[:, sl]
        if rope:
            qh = qh * cs + _rope_partner(qh, HEAD_PAD, MLA_NOPE_DIM) * sn
        q_ref[0, :, sl] = (qh * scale).astype(BF16)
        k_ref[0, :, sl] = (k[:, sl] + kr).astype(BF16)


def _mla_proj(m, qn_w, kvn_w, wq_pad, wk_pad, wv, cos_t, sin_t, rope):
    b, n, w = m.shape
    t = min(1024, n)
    hw = MLA_HEADS * HEAD_PAD
    vw = MLA_HEADS * MLA_V_DIM
    const = lambda shape: pl.BlockSpec(shape, lambda i, j: (0, 0))
    tab = pl.BlockSpec((t, HEAD_PAD), lambda i, j: (j, 0))
    return pl.pallas_call(
        functools.partial(_mla_proj_kernel, rope=rope),
        grid=(b, n // t),
        in_specs=[pl.BlockSpec((1, t, w), lambda i, j: (i, j, 0)),
                  const((1, MLA_Q_RANK)), const((1, MLA_KV_RANK)),
                  const((MLA_Q_RANK, hw)), const((MLA_KV_RANK, hw)), const((MLA_KV_RANK, vw)),
                  tab, tab],
        out_specs=[pl.BlockSpec((1, t, hw), lambda i, j: (i, j, 0)),
                   pl.BlockSpec((1, t, hw), lambda i, j: (i, j, 0)),
                   pl.BlockSpec((1, t, vw), lambda i, j: (i, j, 0))],
        out_shape=[jax.ShapeDtypeStruct((b, n, hw), BF16),
                   jax.ShapeDtypeStruct((b, n, hw), BF16),
                   jax.ShapeDtypeStruct((b, n, vw), BF16)],
        compiler_params=_cp(("parallel", "parallel"), VMEM_LIMIT),
        name="mla_proj",
    )(m, qn_w.reshape(1, -1), kvn_w.reshape(1, -1), wq_pad, wk_pad, wv, cos_t, sin_t)


def _attn_kernel(*refs, n_seg):
    q_ref = refs[0]
    k_refs = refs[1:1 + n_seg]
    v_refs = refs[1 + n_seg:1 + 2 * n_seg]
    o_ref = refs[1 + 2 * n_seg]
    tq = q_ref.shape[1]
    lane = lax.broadcasted_iota(jnp.int32, (tq, 2 * MLA_V_DIM), 1)
    for pair in range(MLA_HEADS // 2):
        outs = []
        for h in (2 * pair, 2 * pair + 1):
            sl = slice(h * HEAD_PAD, (h + 1) * HEAD_PAD)
            qh = q_ref[0, :, sl]
            s = [lax.dot_general(qh, kr[0, :, sl], (((1,), (1,)), ((), ())),
                                 preferred_element_type=F32) for kr in k_refs]
            mx = s[0].max(axis=-1, keepdims=True)
            for si in s[1:]:
                mx = jnp.maximum(mx, si.max(axis=-1, keepdims=True))
            den = jnp.zeros((tq, 1), F32)
            acc = jnp.zeros((tq, 2 * MLA_V_DIM), F32)
            for si, vr in zip(s, v_refs):
                p = jnp.exp2(si - mx)
                den = den + p.sum(axis=-1, keepdims=True)
                acc = acc + _dot(p.astype(BF16), vr[0, :, pair * 2 * MLA_V_DIM:(pair + 1) * 2 * MLA_V_DIM])
            outs.append(acc * (1.0 / den))
        o_ref[0, :, pair * 2 * MLA_V_DIM:(pair + 1) * 2 * MLA_V_DIM] = jnp.where(
            lane < MLA_V_DIM, outs[0], outs[1])


def _attention(q, ks, vs):
    b, nq, hw = q.shape
    tq = min(512, nq)
    vw = MLA_HEADS * MLA_V_DIM
    n_seg = len(ks)
    seg_spec = lambda a: pl.BlockSpec((1,) + a.shape[1:], lambda i, j: (i, 0, 0))
    return pl.pallas_call(
        functools.partial(_attn_kernel, n_seg=n_seg),
        grid=(b, nq // tq),
        in_specs=[pl.BlockSpec((1, tq, hw), lambda i, j: (i, j, 0))]
                 + [seg_spec(a) for a in ks] + [seg_spec(a) for a in vs],
        out_specs=pl.BlockSpec((1, tq, vw), lambda i, j: (i, j, 0)),
        out_shape=jax.ShapeDtypeStruct((b, nq, vw), F32),
        compiler_params=_cp(("parallel", "parallel"), VMEM_LIMIT),
        name="mla_attention",
    )(q, *ks, *vs)


def _store_token_tiles(ref, val):
    t = val.shape[0]
    for j in range(SUBLANES):
        ref[pl.ds(j, t, stride=SUBLANES), :] = val[:, j * LANES:(j + 1) * LANES]


def _load_token_tiles(ref, t):
    return jnp.concatenate([ref[pl.ds(j, t, stride=SUBLANES), :] for j in range(SUBLANES)], axis=1)


def _outproj_kernel(cv_ref, rt_ref, ml_ref, x_ref, g1_ref, wo_ref, nw_ref, sh_ref, sc_ref,
                    rwh_ref, rwl_ref, rb_ref, *rest):
    x1_ref, h2_ref, idx_ref, wt_ref = rest[-4:]
    c0, c1 = CONV_CH, CONV_CH + RET_VW
    y = (_dot(cv_ref[0].astype(BF16), wo_ref[:c0, :])
         + _dot(rt_ref[0].astype(BF16), wo_ref[c0:c1, :])
         + _dot(ml_ref[0].astype(BF16), wo_ref[c1:, :]))
    x1 = x_ref[...] + g1_ref[0] * y
    x1_ref[...] = x1
    h2 = _rms_mod(x1, nw_ref[...], sh_ref[0], sc_ref[0])
    _store_token_tiles(h2_ref, h2)
    h_hi, h_lo = _split_bf16(h2)
    nt_dot = lambda a, bm: lax.dot_general(a, bm, (((1,), (1,)), ((), ())), preferred_element_type=F32)
    logits = nt_dot(rwh_ref[...], h_hi) + nt_dot(rwh_ref[...], h_lo) + nt_dot(rwl_ref[...], h_hi)
    scores = _sigmoid(logits)
    sel = scores + rb_ref[...]
    t = scores.shape[1]
    eio = lax.broadcasted_iota(jnp.int32, (N_EXPERTS, t), 0).astype(F32)
    slot = lax.broadcasted_iota(jnp.int32, (SUBLANES, t), 0)
    idx_out = jnp.zeros((SUBLANES, t), jnp.int32)
    wt_out = jnp.zeros((SUBLANES, t), F32)
    wsum = jnp.zeros((1, t), F32)
    for k in range(TOP_K):
        mx = jnp.max(sel, axis=0, keepdims=True)
        ik = jnp.min(jnp.where(sel == mx, eio, float(N_EXPERTS)), axis=0, keepdims=True)
        hit = eio == ik
        wk = jnp.sum(jnp.where(hit, scores, 0.0), axis=0, keepdims=True)
        sel = jnp.where(hit, -jnp.inf, sel)
        idx_out = jnp.where(slot == k, ik.astype(jnp.int32), idx_out)
        wt_out = jnp.where(slot == k, wk, wt_out)
        wsum = wsum + wk
    idx_ref[...] = idx_out
    wt_ref[...] = wt_out / wsum * ROUTED_SCALE


def _outproj(conv, ret, mla, x_flat, x_row0, g1, wo, nw, sh, sc, rw_hi, rw_lo, rb, n_total, row0, carry):
    b, n, _ = conv.shape
    d = x_flat.shape[1]
    x = x_flat
    t = min(512, n)
    nt = n // t
    off = row0 // t
    x_off = x_row0 // t
    tok = lambda w: pl.BlockSpec((1, t, w), lambda i, j: (i, j, 0))
    per_b = pl.BlockSpec((1, 1, d), lambda i, j: (i, 0, 0))
    const = lambda shape: pl.BlockSpec(shape, lambda i, j: (0, 0))
    flat = lambda rows, w: pl.BlockSpec((rows, w), lambda i, j: (off + i * nt + j, 0))
    x_spec = pl.BlockSpec((t, d), lambda i, j: (x_off + i * nt + j, 0))
    in_specs = [tok(CONV_CH), tok(RET_VW), tok(MLA_HEADS * MLA_V_DIM), x_spec, per_b,
                const((d, d)), const((1, d)), per_b, per_b,
                const((N_EXPERTS, d)), const((N_EXPERTS, d)), const((N_EXPERTS, 1))]
    operands = [conv, ret, mla, x, g1, wo, nw.reshape(1, d), sh, sc, rw_hi, rw_lo, rb.reshape(N_EXPERTS, 1)]
    choice = pl.BlockSpec((SUBLANES, t), lambda i, j: (0, off + i * nt + j))
    aliases = {}
    if carry is not None:
        aliases = {len(operands) + k: k for k in range(len(carry))}
        in_specs += [pl.BlockSpec(memory_space=pl.ANY)] * len(carry)
        operands += list(carry)
    return pl.pallas_call(
        _outproj_kernel,
        grid=(b, nt),
        in_specs=in_specs,
        out_specs=[flat(t, d), flat(t * SUBLANES, LANES), choice, choice],
        out_shape=[jax.ShapeDtypeStruct((n_total, d), F32),
                   jax.ShapeDtypeStruct((n_total * SUBLANES, LANES), F32),
                   jax.ShapeDtypeStruct((SUBLANES, n_total), jnp.int32),
                   jax.ShapeDtypeStruct((SUBLANES, n_total), F32)],
        input_output_aliases=aliases,
        compiler_params=_cp(("parallel", "parallel"), VMEM_LIMIT),
        name="outproj_norm2_router",
    )(*operands)


_MOE_ROWS = 64
_MOE_RMW = 16


def _moe_kernel(be_ref, src_ref, cnt_ref, first_ref, nxt_ref, used_ref, tok_ref, wt_ref,
                h_ref, wg_hbm, wu_hbm, wd_hbm, *rest, blocks_per_tile, dump_row, layer):
    o_ref, xa_ref, xb_ref, yt_ref, wg_s, wu_s, wd_s, sem = rest[-8:]
    tile = pl.program_id(0)
    b0 = tile * blocks_per_tile

    def weight_copies(e, s):
        return (pltpu.make_async_copy(wg_hbm.at[layer, e], wg_s.at[s], sem.at[s, 0]),
                pltpu.make_async_copy(wu_hbm.at[layer, e], wu_s.at[s], sem.at[s, 1]),
                pltpu.make_async_copy(wd_hbm.at[layer, e], wd_s.at[s], sem.at[s, 2]))

    def gather_block(b, dst_ref):
        base = src_ref[b]
        for mi in range(MOE_BLOCK):
            t8 = pl.multiple_of(tok_ref[base + mi], SUBLANES)
            dst_ref[mi * SUBLANES:(mi + 1) * SUBLANES, :] = h_ref[0, pl.ds(t8, SUBLANES), :]

    o_ref[...] = jnp.zeros(o_ref.shape, F32)
    n_used = used_ref[tile]

    @pl.when(n_used > 0)
    def _():
        for cp in weight_copies(be_ref[b0], 0):
            cp.start()
        gather_block(b0, xa_ref)

    def block(i, cur_ref, nxt_buf_ref):
        b = b0 + i
        s = first_ref[b] >> 1

        @pl.when((first_ref[b] & 1) == 1)
        def _():
            for cp in weight_copies(be_ref[b], s):
                cp.wait()

            @pl.when(nxt_ref[b] >= 0)
            def _():
                for cp in weight_copies(nxt_ref[b], 1 - s):
                    cp.start()

        cnt = cnt_ref[b]
        base = src_ref[b]
        last = cnt - 1
        x = jnp.concatenate([cur_ref[pl.ds(j, MOE_BLOCK, stride=SUBLANES), :] for j in range(SUBLANES)],
                            axis=1).astype(BF16)
        gather_block(b0 + jnp.minimum(i + 1, n_used - 1), nxt_buf_ref)
        g = _dot(x, wg_s[s])
        u = _dot(x, wu_s[s])
        y = _dot((_silu(g) * u).astype(BF16), wd_s[s])
        for j in range(SUBLANES):
            yt_ref[pl.ds(j * MOE_STRIDE, MOE_BLOCK), :] = y[:, j * LANES:(j + 1) * LANES]
        def scatter_rows(g0, partial):
            for m0 in range(g0, g0 + _MOE_ROWS, _MOE_RMW):
                rows = []
                for mi in range(m0, m0 + _MOE_RMW):
                    if partial:
                        i = base + jnp.minimum(mi, last)
                        t8 = pl.multiple_of(jnp.where(mi < cnt, tok_ref[i], dump_row), SUBLANES)
                    else:
                        i = base + mi
                        t8 = pl.multiple_of(tok_ref[i], SUBLANES)
                    upd = yt_ref[pl.ds(mi, SUBLANES, stride=MOE_STRIDE), :] * wt_ref[i]
                    rows.append((t8, o_ref[0, pl.ds(t8, SUBLANES), :] + upd))
                for t8, val in rows:
                    o_ref[0, pl.ds(t8, SUBLANES), :] = val

        for g0 in range(0, MOE_BLOCK, _MOE_ROWS):
            pl.when(g0 + _MOE_ROWS <= cnt)(functools.partial(scatter_rows, g0, False))
            pl.when((g0 < cnt) & (g0 + _MOE_ROWS > cnt))(functools.partial(scatter_rows, g0, True))

    def pair(i2, carry):
        i = 2 * i2
        block(i, xa_ref, xb_ref)
        pl.when(i + 1 < n_used)(functools.partial(block, i + 1, xb_ref, xa_ref))
        return carry

    lax.fori_loop(0, (n_used + 1) // 2, pair, 0)


def _moe_tile_size(n_tok):
    for s in (4096, 2048, 1024, 512, 256):
        if n_tok % s == 0:
            return s
    raise ValueError(f"token count {n_tok} must be a multiple of 256")


_MOE_SMEM_WORDS = 64 * 1024


def _routed_experts(h3, idx_t, wts_t, wg, wu, wd, layer, ts):
    n_tiles = h3.shape[0]
    d = SUBLANES * LANES
    n_assign = ts * TOP_K
    bpt = n_assign // MOE_BLOCK + N_EXPERTS

    per_tile = lambda a: a[:TOP_K].reshape(TOP_K, n_tiles, ts).transpose(1, 0, 2).reshape(n_tiles, n_assign)
    e_t = per_tile(idx_t)
    tok8 = (jnp.arange(n_assign, dtype=jnp.int32) % ts) * SUBLANES
    tok_bits = (ts * SUBLANES - 1).bit_length()
    s_key, s_w = lax.sort((e_t * (1 << tok_bits) + tok8, per_tile(wts_t)), dimension=1,
                          num_keys=1, is_stable=False)
    s_tok = s_key & ((1 << tok_bits) - 1)
    counts = jnp.sum((e_t[:, :, None] == jnp.arange(N_EXPERTS, dtype=jnp.int32)).astype(jnp.int32), axis=1)
    start = jnp.cumsum(counts, axis=1) - counts
    nblk = (counts + MOE_BLOCK - 1) // MOE_BLOCK
    blk_end = jnp.cumsum(nblk, axis=1)
    blk_start = blk_end - nblk
    used = blk_end[:, -1:]
    bi = jnp.broadcast_to(jnp.arange(bpt, dtype=jnp.int32), (n_tiles, bpt))
    bi_c = jnp.minimum(bi, used - 1)
    e_b = jnp.sum((blk_end[:, None, :] <= bi_c[:, :, None]).astype(jnp.int32), axis=2)
    is_e = e_b[:, :, None] == jnp.arange(N_EXPERTS, dtype=jnp.int32)
    take = lambda a: jnp.sum(jnp.where(is_e, a[:, None, :], 0), axis=2)
    j = bi_c - take(blk_start)
    src = take(start) + j * MOE_BLOCK
    cnt = jnp.clip(take(counts) - j * MOE_BLOCK, 0, MOE_BLOCK)
    e_ids = jnp.arange(N_EXPERTS, dtype=jnp.int32)
    has = nblk > 0
    ordinal = jnp.cumsum(has.astype(jnp.int32), axis=1) - 1
    later = jnp.where(has, e_ids, N_EXPERTS)
    later = jnp.concatenate([later[:, 1:], jnp.full((n_tiles, 1), N_EXPERTS, jnp.int32)], axis=1)
    nxt_e = lax.cummin(later, axis=1, reverse=True)
    nxt_e = jnp.where(nxt_e == N_EXPERTS, -1, nxt_e)
    first = (take(ordinal) % 2) * 2 + (j == 0).astype(jnp.int32)
    nxt = take(nxt_e)

    n_calls = -(-n_tiles // (_MOE_SMEM_WORDS // n_assign))
    bounds = [n_tiles * c // n_calls for c in range(n_calls + 1)]
    out = None
    n_pref = 8
    for c in range(n_calls):
        t0, t1 = bounds[c], bounds[c + 1]
        group = t1 - t0
        src_abs = src[t0:t1] + jnp.arange(group, dtype=jnp.int32)[:, None] * n_assign
        tile_map = lambda i, *_, t0=t0: (t0 + i, 0, 0)
        hbm = pl.BlockSpec(memory_space=pl.ANY)
        in_specs = [pl.BlockSpec((1, ts * SUBLANES, LANES), tile_map, pipeline_mode=pl.Buffered(1)),
                    hbm, hbm, hbm]
        operands = [e_b[t0:t1].reshape(-1), src_abs.reshape(-1), cnt[t0:t1].reshape(-1),
                    first[t0:t1].reshape(-1), nxt[t0:t1].reshape(-1), used[t0:t1].reshape(-1),
                    jnp.pad(s_tok[t0:t1].reshape(-1), (0, MOE_BLOCK)), s_w[t0:t1].reshape(-1), h3, wg, wu, wd]
        aliases = {}
        if out is not None:
            in_specs.append(hbm)
            aliases = {len(operands): 0}
            operands.append(out)
        grid_spec = pltpu.PrefetchScalarGridSpec(
            num_scalar_prefetch=n_pref,
            grid=(group,),
            in_specs=in_specs,
            out_specs=pl.BlockSpec((1, (ts + 1) * SUBLANES, LANES), tile_map, pipeline_mode=pl.Buffered(1)),
            scratch_shapes=[pltpu.VMEM((MOE_BLOCK * SUBLANES, LANES), F32),
                            pltpu.VMEM((MOE_BLOCK * SUBLANES, LANES), F32),
                            pltpu.VMEM((SUBLANES * MOE_STRIDE, LANES), F32),
                            pltpu.VMEM((2, d, EXPERT_DIM), BF16),
                            pltpu.VMEM((2, d, EXPERT_DIM), BF16),
                            pltpu.VMEM((2, EXPERT_DIM, d), BF16),
                            pltpu.SemaphoreType.DMA((2, 3))],
        )
        out = pl.pallas_call(
            functools.partial(_moe_kernel, blocks_per_tile=bpt, dump_row=ts * SUBLANES, layer=layer),
            grid_spec=grid_spec,
            out_shape=jax.ShapeDtypeStruct((n_tiles, (ts + 1) * SUBLANES, LANES), F32),
            input_output_aliases=aliases,
            compiler_params=_cp(("arbitrary",), VMEM_LIMIT),
            name="routed_experts",
        )(*operands)
    return out


def _ffn_out_kernel(x_ref, h_ref, r_ref, g2_ref, sg_ref, su_ref, sd_ref, fw_ref, o_ref, *, final):
    t = x_ref.shape[0]
    h = _load_token_tiles(h_ref.at[0], t).astype(BF16)
    a = _silu(_dot(h, sg_ref[...])) * _dot(h, su_ref[...])
    y = _load_token_tiles(r_ref.at[0], t) + _dot(a.astype(BF16), sd_ref[...])
    x2 = x_ref[...] + g2_ref[0] * y
    if final:
        var = jnp.mean(x2 * x2, axis=-1, keepdims=True)
        x2 = (x2 * lax.rsqrt(var + NORM_EPS)) * fw_ref[...]
    o_ref[...] = x2


def _ffn_out(x1, h3, routed, ts, g2_rows, rows_per_gate, sg, su, sd, fw, final):
    n_tok, d = x1.shape
    t = next(c for c in (1024, 512, 256) if c == 256 or (rows_per_gate % c == 0 and n_tok % c == 0 and ts % c == 0))
    tiles_per_row = rows_per_gate // t
    per = ts // t
    last = g2_rows.shape[0] - 1
    tok = pl.BlockSpec((t, d), lambda i: (i, 0))
    tiles = pl.BlockSpec((1, t * SUBLANES, LANES), lambda i: (i // per, i % per, 0))
    const = lambda shape: pl.BlockSpec(shape, lambda i: (0, 0))
    return pl.pallas_call(
        functools.partial(_ffn_out_kernel, final=final),
        grid=(n_tok // t,),
        in_specs=[tok, tiles, tiles,
                  pl.BlockSpec((1, 1, d), lambda i: (jnp.minimum(i // tiles_per_row, last), 0, 0)),
                  const((d, EXPERT_DIM)), const((d, EXPERT_DIM)), const((EXPERT_DIM, d)), const((1, d))],
        out_specs=tok,
        out_shape=jax.ShapeDtypeStruct((n_tok, d), F32),
        compiler_params=_cp(("parallel",), VMEM_LIMIT),
        name="shared_expert_residual",
    )(x1, h3, routed, g2_rows, sg, su, sd, fw.reshape(1, d))


def _rope_tables(rows, dim, group, lo):
    pos_r = jnp.repeat(jnp.arange(rows, dtype=F32), GRID_W)
    pos_c = jnp.tile(jnp.arange(GRID_W, dtype=F32), rows)
    n_freq = dim // 4
    inv = ROPE_BASE ** (-jnp.arange(n_freq, dtype=F32) / n_freq)
    ang = jnp.concatenate([pos_r[:, None] * inv, pos_c[:, None] * inv], axis=-1)
    cos, sin = jnp.cos(ang), jnp.sin(ang)
    n = rows * GRID_W
    half = dim // 2
    cos_g = jnp.ones((n, group), F32).at[:, lo:lo + dim].set(jnp.concatenate([cos, cos], axis=-1))
    sin_g = jnp.zeros((n, group), F32).at[:, lo:lo + dim].set(jnp.concatenate([-sin, sin], axis=-1))
    reps = LANES // group
    return jnp.tile(cos_g, (1, reps)), jnp.tile(sin_g, (1, reps))


def _pad_in_proj(w_in):
    d = w_in.shape[0]
    body = w_in[:, :IN_COLS_PAD - HEAD_PAD]
    kr = w_in[:, IN_COLS_PAD - HEAD_PAD:]
    kr_pad = jnp.zeros((d, HEAD_PAD), w_in.dtype).at[:, MLA_NOPE_DIM:MLA_NOPE_DIM + MLA_ROPE_DIM].set(kr)
    return jnp.concatenate([body, kr_pad], axis=1).astype(BF16)


def _pad_heads(w, width):
    k = w.shape[0]
    w3 = w.reshape(k, MLA_HEADS, width)
    return jnp.zeros((k, MLA_HEADS, HEAD_PAD), w.dtype).at[:, :, :width].set(w3).reshape(
        k, MLA_HEADS * HEAD_PAD).astype(BF16)


def kernel(x, c, ctx, c_ctx, mod_w, mod_b, norm1_w, w_in, conv_w, conv_b, conv_ln_w, conv_ln_b,
           ret_decay_logit, ret_gn_w, q_norm_w, w_uq, kv_norm_w, w_ukv, w_out, norm2_w,
           router_w, router_b, exp_w_gate, exp_w_up, exp_w_down, sh_w_gate, sh_w_up, sh_w_down,
           final_norm_w):
    b, n_lat, d = x.shape
    n_ctx = ctx.shape[1]
    depth = mod_w.shape[0]
    rows = n_lat // GRID_W
    cos_ret, sin_ret = _rope_tables(rows, RET_QK_DIM, RET_QK_DIM, 0)
    cos_mla, sin_mla = _rope_tables(rows, MLA_ROPE_DIM, HEAD_PAD, MLA_NOPE_DIM)

    mod_rows = -(-(b + 1) // SUBLANES) * SUBLANES
    cc = jnp.zeros((mod_rows, d), F32).at[:b].set(c).at[b].set(c_ctx)

    n_l = b * n_lat
    wg_all, wu_all, wd_all = exp_w_gate.astype(BF16), exp_w_up.astype(BF16), exp_w_down.astype(BF16)
    xl, xl_row = x.reshape(n_l, d), 0
    xc, xc_row = ctx.reshape(b * n_ctx, d), 0
    for i in range(depth):
        last = i == depth - 1
        mod = _modulation(cc, mod_w[i], mod_b[i])
        ml = mod[:b].reshape(b, 1, 6, d)
        sh1, sc1, g1, sh2, sc2, g2 = [ml[:, :, j, :] for j in range(6)]
        mc = jnp.broadcast_to(mod[b].reshape(1, 1, 6, d), (b, 1, 6, d))
        csh1, csc1, cg1, csh2, csc2, cg2 = [mc[:, :, j, :] for j in range(6)]

        w_in_p = _pad_in_proj(w_in[i])
        ul, rl, mlat = _inproj(xl, xl_row, b, n_lat, norm1_w[i], sh1, sc1, w_in_p)
        uc, rc, mctx = _inproj(xc, xc_row, b, n_ctx, norm1_w[i], csh1, csc1, w_in_p)

        conv_l = _conv(ul, conv_w[i], conv_b[i], conv_ln_w[i], conv_ln_b[i])
        log_gamma = jax.nn.log_sigmoid(ret_decay_logit[i].astype(F32))
        ret_l, ret_c = _retention(rl, rc, log_gamma, cos_ret, sin_ret, ret_gn_w[i], not last)

        wq_p = _pad_heads(w_uq[i], MLA_NOPE_DIM + MLA_ROPE_DIM)
        wkv = w_ukv[i].reshape(MLA_KV_RANK, MLA_HEADS, MLA_NOPE_DIM + MLA_V_DIM)
        wk_p = _pad_heads(wkv[:, :, :MLA_NOPE_DIM].reshape(MLA_KV_RANK, -1), MLA_NOPE_DIM)
        wv = wkv[:, :, MLA_NOPE_DIM:].reshape(MLA_KV_RANK, -1).astype(BF16)
        ql, kl, vl = _mla_proj(mlat, q_norm_w[i], kv_norm_w[i], wq_p, wk_p, wv, cos_mla, sin_mla, True)
        qc, kc, vc = _mla_proj(mctx, q_norm_w[i], kv_norm_w[i], wq_p, wk_p, wv,
                               cos_mla[:n_ctx], sin_mla[:n_ctx], False)
        mla_l = _attention(ql, [kc, kl], [vc, vl])

        wo = w_out[i].astype(BF16)
        rw_t = router_w[i].T
        rw_hi = rw_t.astype(BF16)
        rw_lo = (rw_t - rw_hi.astype(F32)).astype(BF16)
        n_tok = n_l if last else n_l + b * n_ctx
        outs = _outproj(conv_l, ret_l, mla_l, xl, xl_row, g1, wo, norm2_w[i], sh2, sc2,
                        rw_hi, rw_lo, router_b[i], n_tok, 0, None)
        g2_rows = g2
        if not last:
            conv_c = _conv(uc, conv_w[i], conv_b[i], conv_ln_w[i], conv_ln_b[i])
            mla_c = _attention(qc, [kc], [vc])
            outs = _outproj(conv_c, ret_c, mla_c, xc, xc_row, cg1, wo, norm2_w[i], csh2, csc2,
                            rw_hi, rw_lo, router_b[i], n_tok, n_l, outs)
            g2_rows = jnp.concatenate([g2, cg2[:1]], axis=0)
        x1, h2t, idx, wts = outs

        ts = _moe_tile_size(n_tok)
        h3 = h2t.reshape(n_tok // ts, ts * SUBLANES, LANES)
        routed = _routed_experts(h3, idx, wts, wg_all, wu_all, wd_all, i, ts)
        x2 = _ffn_out(x1, h3, routed, ts, g2_rows, n_lat, sh_w_gate[i].astype(BF16),
                      sh_w_up[i].astype(BF16), sh_w_down[i].astype(BF16), final_norm_w, last)
        xl, xl_row = x2, 0
        xc, xc_row = x2, n_l
    return xl.reshape(b, n_lat, d)
```
